```python
import jax, jax.numpy as jnp
from jax import lax
import numpy as np

D_MODEL = 2048
BATCH = 8
SEQ = 2048
DEPTH = 2
DEC_BATCH = 32
DEC_SEQ = 4
PAST_LEN = 8192
PAGE_SIZE = 128

N_CONV_LAYERS = (DEPTH + 1) // 2
N_ATTN_LAYERS = DEPTH // 2
D_A = D_MODEL // 2
D_B = D_MODEL // 2
K_A = 31
K_B = 3
N_HEADS = 16
HEAD_DIM = D_MODEL // N_HEADS
DIL_PATTERNS = ((128, 1), (512, 4), (2048, 16))
MAX_WINDOW = 2048
BLK = 128
D_FF = 5632
N_MOD = 9
EPS = 1e-6
NEG = -1e30

kernel_name = 'hybrid_conv_dilated_alibi_decode_step'


def rmsnorm(x, g):
    x32 = x.astype(jnp.float32)
    y = x32 * lax.rsqrt(jnp.mean(x32 * x32, axis=-1, keepdims=True) + EPS)
    return (y * g.astype(jnp.float32)).astype(x.dtype)


def layernorm(x, g, b):
    x32 = x.astype(jnp.float32)
    mu = jnp.mean(x32, axis=-1, keepdims=True)
    xc = x32 - mu
    y = xc * lax.rsqrt(jnp.mean(xc * xc, axis=-1, keepdims=True) + EPS)
    return (y * g.astype(jnp.float32) + b.astype(jnp.float32)).astype(x.dtype)


def modulate(x, g, shift, scale):
    return rmsnorm(x, g) * (1 + scale[:, None, :]) + shift[:, None, :]


def swiglu(h, wg, wu, wd):
    return (jax.nn.silu(h @ wg) * (h @ wu)) @ wd


def alibi_slopes():
    return jnp.exp2(-8.0 * jnp.arange(1, N_HEADS + 1, dtype=jnp.float32) / N_HEADS)


def causal_dwconv(u_full, w):
    c = u_full.shape[-1]
    return lax.conv_general_dilated(u_full, w[:, None, :].astype(u_full.dtype), window_strides=(1,),
                                    padding='VALID', dimension_numbers=('NWC', 'WIO', 'NWC'),
                                    feature_group_count=c)


def conv_mixer(h, buf_a, buf_b, w_in, dw_a, b_a, ln_g, ln_b, dw_b, w_out):
    proj = h @ w_in
    a_val, a_gate, g_b, g_c, z = jnp.split(proj, [D_A, 2 * D_A, 2 * D_A + D_B, 2 * D_A + 2 * D_B], axis=-1)
    u = a_val * jax.nn.sigmoid(a_gate)
    u_full = jnp.concatenate([buf_a.astype(u.dtype), u], axis=1)
    ya = causal_dwconv(u_full, dw_a) + b_a
    ya = jax.nn.silu(layernorm(ya, ln_g, ln_b))
    zc = g_c * z
    z_full = jnp.concatenate([buf_b.astype(zc.dtype), zc], axis=1)
    yb = g_b * causal_dwconv(z_full, dw_b)
    out = jnp.concatenate([ya, yb], axis=-1) @ w_out
    return out, u_full[:, -(K_A - 1):], z_full[:, -(K_B - 1):]


def lse_attend(scores, v, eq):
    m = jnp.max(scores, axis=-1, keepdims=True)
    p = jnp.exp(scores - m)
    s = jnp.sum(p, axis=-1, keepdims=True)
    o = jnp.einsum(eq, (p / s).astype(v.dtype), v)
    return o, (m + jnp.log(s))[..., 0]


def dilated_prompt(q, k, v, window, dil, slopes):
    n, s, hh, dh = q.shape
    n_taps = window // dil
    sub_len = -(-s // dil)
    nb = -(-sub_len // BLK)
    lp = nb * BLK
    sp = lp * dil

    def split(a):
        a = jnp.pad(a, ((0, 0), (0, sp - s), (0, 0), (0, 0)))
        return a.reshape(n, lp, dil, hh, dh).transpose(0, 2, 1, 3, 4).reshape(n, dil, nb, BLK, hh, dh)

    def with_prev(a):
        prev = jnp.pad(a, ((0, 0), (0, 0), (1, 0), (0, 0), (0, 0), (0, 0)))[:, :, :-1]
        return jnp.concatenate([prev, a], axis=3)

    qb = split(q)
    kk = with_prev(split(k))
    vv = with_prev(split(v))
    scores = jnp.einsum('brnqhd,brnkhd->brnhqk', qb, kk, preferred_element_type=jnp.float32) * (dh ** -0.5)
    uq = jnp.arange(BLK)[:, None] + BLK
    uk = jnp.arange(2 * BLK)[None, :]
    dist = uq - uk
    key_ok = (jnp.arange(nb)[:, None, None] * BLK + uk - BLK) >= 0
    valid = (dist >= 0) & (dist <= n_taps) & key_ok
    bias = -slopes[:, None, None] * (dist * dil).astype(jnp.float32)
    scores = jnp.where(valid[:, None], scores + bias, NEG)
    o, lse = lse_attend(scores, vv, 'brnhqk,brnkhd->brnqhd')
    o = o.reshape(n, dil, lp, hh, dh).transpose(0, 2, 1, 3, 4).reshape(n, sp, hh, dh)[:, :s]
    lse = lse.transpose(0, 1, 2, 4, 3).reshape(n, dil, lp, hh).transpose(0, 2, 1, 3).reshape(n, sp, hh)[:, :s]
    return o, lse


def dilated_sample(q, k_all, v_all, pos_q, base, window, dil, slopes):
    n_taps = window // dil
    j = jnp.arange(n_taps + 1)
    pos_k = pos_q[:, None] - j[None, :] * dil
    row = pos_k - base
    valid = row >= 0
    row = jnp.clip(row, 0, k_all.shape[1] - 1)
    kg = k_all[:, row]
    vg = v_all[:, row]
    scores = jnp.einsum('nthd,ntjhd->nhtj', q, kg, preferred_element_type=jnp.float32) * (q.shape[-1] ** -0.5)
    bias = -slopes[:, None, None] * (j * dil).astype(jnp.float32)[None, None, :]
    scores = jnp.where(valid[None], scores + bias, NEG)
    o, lse = lse_attend(scores, vg, 'nhtj,ntjhd->nthd')
    return o, lse.transpose(0, 2, 1)


def combine_patterns(outs, lses, dtype):
    w = jax.nn.softmax(jnp.stack(lses, axis=0), axis=0)
    return jnp.einsum('gnth,gnthd->nthd', w.astype(dtype), jnp.stack(outs, axis=0)).astype(dtype)


def attn_prompt(h, w_qkv, w_o):
    n, s, _ = h.shape
    qkv = (h @ w_qkv).reshape(n, s, 3, N_HEADS, HEAD_DIM)
    q, k, v = qkv[:, :, 0], qkv[:, :, 1], qkv[:, :, 2]
    slopes = alibi_slopes()
    res = [dilated_prompt(q, k, v, win, dil, slopes) for (win, dil) in DIL_PATTERNS]
    o = combine_patterns([r[0] for r in res], [r[1] for r in res], h.dtype)
    out = o.reshape(n, s, N_HEADS * HEAD_DIM) @ w_o
    wb = min(MAX_WINDOW, s)
    return out, k[:, s - wb:], v[:, s - wb:]


def attn_sample(h, cache_k, cache_v, w_qkv, w_o):
    n, t, _ = h.shape
    wb = cache_k.shape[1]
    qkv = (h @ w_qkv).reshape(n, t, 3, N_HEADS, HEAD_DIM)
    q, k, v = qkv[:, :, 0], qkv[:, :, 1], qkv[:, :, 2]
    k_all = jnp.concatenate([cache_k.astype(k.dtype), k], axis=1)
    v_all = jnp.concatenate([cache_v.astype(v.dtype), v], axis=1)
    pos_q = PAST_LEN + jnp.arange(t)
    base = PAST_LEN - wb
    slopes = alibi_slopes()
    res = [dilated_sample(q, k_all, v_all, pos_q, base, win, dil, slopes) for (win, dil) in DIL_PATTERNS]
    o = combine_patterns([r[0] for r in res], [r[1] for r in res], h.dtype)
    out = o.reshape(n, t, N_HEADS * HEAD_DIM) @ w_o
    return out, k, v


def trunk(x, c, buf_a, buf_b, kv_k, kv_v, is_sample, w_mod, b_mod, norm_g, ffn_w_gate, ffn_w_up, ffn_w_down,
          conv_w_in, conv_dw_a, conv_b_a, conv_ln_g, conv_ln_b, conv_dw_b, conv_w_out, attn_w_qkv, attn_w_o):
    n = x.shape[0]
    new_a, new_b, new_k, new_v = [], [], [], []
    for layer in range(DEPTH):
        mod = jax.nn.silu(c) @ w_mod[layer] + b_mod[layer]
        sh1, sc1, gt1, sh2, sc2, gt2, sh3, sc3, gt3 = jnp.split(mod, N_MOD, axis=-1)
        g = norm_g[layer]
        hh = modulate(x, g[0], sh1, sc1)
        f = swiglu(hh, ffn_w_gate[layer, 0], ffn_w_up[layer, 0], ffn_w_down[layer, 0])
        x = x + 0.5 * gt1[:, None, :] * rmsnorm(f, g[1])
        hh = modulate(x, g[2], sh2, sc2)
        li = layer // 2
        if layer % 2 == 0:
            if is_sample:
                ba, bb = buf_a[li], buf_b[li]
            else:
                ba = jnp.zeros((n, K_A - 1, D_A), x.dtype)
                bb = jnp.zeros((n, K_B - 1, D_B), x.dtype)
            out, sa, sb = conv_mixer(hh, ba, bb, conv_w_in[li], conv_dw_a[li], conv_b_a[li], conv_ln_g[li],
                                     conv_ln_b[li], conv_dw_b[li], conv_w_out[li])
            new_a.append(sa)
            new_b.append(sb)
        else:
            if is_sample:
                out, sk, sv = attn_sample(hh, kv_k[li], kv_v[li], attn_w_qkv[li], attn_w_o[li])
            else:
                out, sk, sv = attn_prompt(hh, attn_w_qkv[li], attn_w_o[li])
            new_k.append(sk)
            new_v.append(sv)
        x = x + gt2[:, None, :] * rmsnorm(out, g[3])
        hh = modulate(x, g[4], sh3, sc3)
        f = swiglu(hh, ffn_w_gate[layer, 1], ffn_w_up[layer, 1], ffn_w_down[layer, 1])
        x = x + 0.5 * gt3[:, None, :] * rmsnorm(f, g[5])
    return x, jnp.stack(new_a), jnp.stack(new_b), jnp.stack(new_k), jnp.stack(new_v)


def setup_inputs(seed: int = 0) -> dict:
    key = jax.random.key(seed)
    ks = jax.random.split(key, 24)

    def nrm(k, shape, s):
        return jax.random.normal(k, shape, jnp.float32) * s

    wb = min(MAX_WINDOW, PAST_LEN)
    return {
        'x_prompt': nrm(ks[0], (BATCH, SEQ, D_MODEL), 1.0),
        'x_sample': nrm(ks[1], (DEC_BATCH, DEC_SEQ, D_MODEL), 1.0),
        'cache_conv_a': nrm(ks[2], (N_CONV_LAYERS, DEC_BATCH, K_A - 1, D_A), 0.5),
        'cache_conv_b': nrm(ks[3], (N_CONV_LAYERS, DEC_BATCH, K_B - 1, D_B), 0.5),
        'cache_k': nrm(ks[4], (N_ATTN_LAYERS, DEC_BATCH, wb, N_HEADS, HEAD_DIM), 1.0),
        'cache_v': nrm(ks[5], (N_ATTN_LAYERS, DEC_BATCH, wb, N_HEADS, HEAD_DIM), 1.0),
        'c_prompt': nrm(ks[6], (BATCH, D_MODEL), 1.0),
        'c_sample': nrm(ks[7], (DEC_BATCH, D_MODEL), 1.0),
        'w_mod': nrm(ks[8], (DEPTH, D_MODEL, N_MOD * D_MODEL), D_MODEL ** -0.5),
        'b_mod': nrm(ks[9], (DEPTH, N_MOD * D_MODEL), 0.02),
        'norm_g': 1.0 + nrm(ks[10], (DEPTH, 6, D_MODEL), 0.05),
        'ffn_w_gate': nrm(ks[11], (DEPTH, 2, D_MODEL, D_FF), D_MODEL ** -0.5),
        'ffn_w_up': nrm(ks[12], (DEPTH, 2, D_MODEL, D_FF), D_MODEL ** -0.5),
        'ffn_w_down': nrm(ks[13], (DEPTH, 2, D_FF, D_MODEL), D_FF ** -0.5),
        'conv_w_in': nrm(ks[14], (N_CONV_LAYERS, D_MODEL, 2 * D_A + 3 * D_B), D_MODEL ** -0.5),
        'conv_dw_a': nrm(ks[15], (N_CONV_LAYERS, K_A, D_A), K_A ** -0.5),
        'conv_b_a': nrm(ks[16], (N_CONV_LAYERS, D_A), 0.02),
        'conv_ln_g': 1.0 + nrm(ks[17], (N_CONV_LAYERS, D_A), 0.05),
        'conv_ln_b': nrm(ks[18], (N_CONV_LAYERS, D_A), 0.02),
        'conv_dw_b': nrm(ks[19], (N_CONV_LAYERS, K_B, D_B), K_B ** -0.5),
        'conv_w_out': nrm(ks[20], (N_CONV_LAYERS, D_A + D_B, D_MODEL), (D_A + D_B) ** -0.5),
        'attn_w_qkv': nrm(ks[21], (N_ATTN_LAYERS, D_MODEL, 3 * N_HEADS * HEAD_DIM), D_MODEL ** -0.5),
        'attn_w_o': nrm(ks[22], (N_ATTN_LAYERS, N_HEADS * HEAD_DIM, D_MODEL), (N_HEADS * HEAD_DIM) ** -0.5),
    }


def reference(x_prompt, x_sample, cache_conv_a, cache_conv_b, cache_k, cache_v, c_prompt, c_sample,
              w_mod, b_mod, norm_g, ffn_w_gate, ffn_w_up, ffn_w_down, conv_w_in, conv_dw_a, conv_b_a,
              conv_ln_g, conv_ln_b, conv_dw_b, conv_w_out, attn_w_qkv, attn_w_o):
    y_prompt, conv_a_prompt, conv_b_prompt, k_prompt, v_prompt = trunk(
        x_prompt, c_prompt, None, None, None, None, False,
        w_mod, b_mod, norm_g, ffn_w_gate, ffn_w_up, ffn_w_down, conv_w_in, conv_dw_a, conv_b_a,
        conv_ln_g, conv_ln_b, conv_dw_b, conv_w_out, attn_w_qkv, attn_w_o)
    y_sample, conv_a_sample, conv_b_sample, k_sample, v_sample = trunk(
        x_sample, c_sample, cache_conv_a, cache_conv_b, cache_k, cache_v, True,
        w_mod, b_mod, norm_g, ffn_w_gate, ffn_w_up, ffn_w_down, conv_w_in, conv_dw_a, conv_b_a,
        conv_ln_g, conv_ln_b, conv_dw_b, conv_w_out, attn_w_qkv, attn_w_o)
    return (y_prompt, y_sample, conv_a_prompt, conv_a_sample, conv_b_prompt, conv_b_sample,
            k_prompt, v_prompt, k_sample, v_sample)
```

```python
import functools
import math

import jax
import jax.numpy as jnp
from jax import lax
from jax.experimental import pallas as pl
from jax.experimental.pallas import tpu as pltpu

F32 = jnp.float32
BF16 = jnp.bfloat16

HEAD_DIM = 128
DIL_PATTERNS = ((128, 1), (512, 4), (2048, 16))
N_MOD = 9
EPS = 1e-6
NEG = -1e30

V7X_VMEM_BYTES = 64 * 1024 * 1024
VMEM_LIMIT_BYTES = V7X_VMEM_BYTES - 8 * 1024 * 1024


def _params(n_axes):
    return pltpu.CompilerParams(
        dimension_semantics=("arbitrary",) * n_axes,
        vmem_limit_bytes=VMEM_LIMIT_BYTES,
    )


def _rms(x, g):
    return x * lax.rsqrt(jnp.mean(x * x, axis=-1, keepdims=True) + EPS) * g


def _silu(x):
    return x * jax.nn.sigmoid(x)


def _dot(a, b):
    return jnp.dot(a, b, preferred_element_type=F32)


def _dot_nt(a, b):
    return lax.dot_general(a, b, (((1,), (1,)), ((), ())), preferred_element_type=F32)


def _mod_kernel(c_ref, w_ref, b_ref, o_ref):
    a = _silu(c_ref[...]).astype(BF16)
    o_ref[...] = _dot(a, w_ref[...].astype(BF16)) + b_ref[...]


def _mod_table(c_all, w_mod, b_mod, tn=1024):
    n_layers, d, n9 = w_mod.shape
    r = c_all.shape[0]
    return pl.pallas_call(
        _mod_kernel,
        out_shape=jax.ShapeDtypeStruct((n_layers, r, n9), F32),
        grid=(n_layers, n9 // tn),
        in_specs=[
            pl.BlockSpec((r, d), lambda l, j: (0, 0)),
            pl.BlockSpec((None, d, tn), lambda l, j: (l, 0, j)),
            pl.BlockSpec((None, 1, tn), lambda l, j: (l, 0, j)),
        ],
        out_specs=pl.BlockSpec((None, r, tn), lambda l, j: (l, 0, j)),
        compiler_params=_params(2),
        name="mod_table",
    )(c_all, w_mod, b_mod.reshape(n_layers, 1, n9))


class _Mod:
    def __init__(self, table, per_row, layer, sub, norm_g4):
        self.table = table
        self.per_row = per_row
        self.layer = layer
        self.sub = sub
        self.norm_g4 = norm_g4

    def spec(self, which, tm, rows_per_group):
        l, idx = self.layer, 3 * self.sub + which
        d = self.table.shape[-1]
        if self.per_row:
            return pl.BlockSpec((None, None, tm, d), lambda i, *_: (l, idx, i, 0))
        return pl.BlockSpec((None, None, None, 1, d),
                            lambda i, *_: (l, (i * tm) // rows_per_group, idx, 0, 0))

    def gain_spec(self, which):
        l, idx = self.layer, 2 * self.sub + which
        d = self.norm_g4.shape[-1]
        return pl.BlockSpec((None, None, 1, d), lambda i, *_: (l, idx, 0, 0))


def _ffn_kernel(x_ref, sh_ref, sc_ref, gt_ref, gin_ref, gout_ref, wg_ref, wu_ref, wd_ref,
                o_ref, h_ref):
    j = pl.program_id(1)

    @pl.when(j == 0)
    def _():
        h = _rms(x_ref[...], gin_ref[...]) * (1.0 + sc_ref[...]) + sh_ref[...]
        h_ref[...] = h.astype(BF16)

    h = h_ref[...]
    g = _dot(h, wg_ref[...])
    u = _dot(h, wu_ref[...])
    part = _dot((_silu(g) * u).astype(BF16), wd_ref[...])

    @pl.when(j == 0)
    def _():
        o_ref[...] = part

    @pl.when(j > 0)
    def _():
        o_ref[...] += part

    @pl.when(j == pl.num_programs(1) - 1)
    def _():
        o_ref[...] = x_ref[...] + 0.5 * gt_ref[...] * _rms(o_ref[...], gout_ref[...])


def _ffn(x, mod, wg, wu, wd, half, rows_per_group, tm, tf):
    m, d = x.shape
    f = wg.shape[-1]
    l = mod.layer
    row = pl.BlockSpec((tm, d), lambda i, j: (i, 0))
    return pl.pallas_call(
        _ffn_kernel,
        out_shape=jax.ShapeDtypeStruct((m, d), F32),
        grid=(m // tm, f // tf),
        in_specs=[
            row,
            mod.spec(0, tm, rows_per_group), mod.spec(1, tm, rows_per_group),
            mod.spec(2, tm, rows_per_group),
            mod.gain_spec(0), mod.gain_spec(1),
            pl.BlockSpec((None, None, d, tf), lambda i, j: (l, half, 0, j)),
            pl.BlockSpec((None, None, d, tf), lambda i, j: (l, half, 0, j)),
            pl.BlockSpec((None, None, tf, d), lambda i, j: (l, half, j, 0)),
        ],
        out_specs=row,
        scratch_shapes=[pltpu.VMEM((tm, d), BF16)],
        compiler_params=_params(2),
        name="ffn",
    )(x, mod.table, mod.table, mod.table, mod.norm_g4, mod.norm_g4, wg, wu, wd)


def _inproj_kernel(x_ref, sh_ref, sc_ref, gin_ref, w_ref, *o_refs):
    h = _rms(x_ref[...], gin_ref[...]) * (1.0 + sc_ref[...]) + sh_ref[...]
    h = h.astype(BF16)
    for k, o_ref in enumerate(o_refs):
        tn = o_ref.shape[-1]
        o_ref[...] = _dot(h, w_ref[:, k * tn:(k + 1) * tn])


def _inproj(x, mod, w, li, n_out, rows_per_group, tm):
    m, d = x.shape
    n_cols = w.shape[-1]
    tn = n_cols // n_out
    return pl.pallas_call(
        _inproj_kernel,
        out_shape=[jax.ShapeDtypeStruct((m, tn), F32)] * n_out,
        grid=(m // tm,),
        in_specs=[
            pl.BlockSpec((tm, d), lambda i: (i, 0)),
            mod.spec(0, tm, rows_per_group), mod.spec(1, tm, rows_per_group),
            mod.gain_spec(0),
            pl.BlockSpec((None, d, n_cols), lambda i: (li, 0, 0), pipeline_mode=pl.Buffered(1)),
        ],
        out_specs=[pl.BlockSpec((tm, tn), lambda i: (i, 0))] * n_out,
        compiler_params=_params(1),
        name="inproj",
    )(x, mod.table, mod.table, mod.norm_g4, w)


def _outproj_kernel(y_ref, x_ref, gt_ref, gout_ref, w_ref, o_ref):
    o = _dot(y_ref[...].astype(BF16), w_ref[...])
    o_ref[...] = x_ref[...] + gt_ref[...] * _rms(o, gout_ref[...])


def _outproj(y, x, mod, w, li, rows_per_group, tm):
    m, d = x.shape
    k = y.shape[-1]
    return pl.pallas_call(
        _outproj_kernel,
        out_shape=jax.ShapeDtypeStruct((m, d), F32),
        grid=(m // tm,),
        in_specs=[
            pl.BlockSpec((tm, k), lambda i: (i, 0)),
            pl.BlockSpec((tm, d), lambda i: (i, 0)),
            mod.spec(2, tm, rows_per_group),
            mod.gain_spec(1),
            pl.BlockSpec((None, k, d), lambda i: (li, 0, 0), pipeline_mode=pl.Buffered(1)),
        ],
        out_specs=pl.BlockSpec((tm, d), lambda i: (i, 0)),
        compiler_params=_params(1),
        name="outproj",
    )(y, x, mod.table, mod.norm_g4, w)


CONV_ROW_CHUNK = 32
CONV_COL_CHUNK = 512


def _conv_kernel(*refs, ka, kb, has_hist):
    if has_hist:
        (av_ref, ag_ref, gb_ref, gc_ref, z_ref, ha_ref, hb_ref,
         dwa_ref, ba_ref, lng_ref, lnb_ref, dwb_ref,
         y_ref, ca_ref, cb_ref, ubuf, zbuf, cbuf) = refs
    else:
        (av_ref, ag_ref, gb_ref, gc_ref, z_ref,
         dwa_ref, ba_ref, lng_ref, lnb_ref, dwb_ref,
         y_ref, ca_ref, cb_ref, ubuf, zbuf, cbuf) = refs
    s = pl.program_id(1)
    ts, c = av_ref.shape
    pa = ubuf.shape[0] - ts
    pb = zbuf.shape[0] - ts

    @pl.when(s == 0)
    def _():
        if has_hist:
            ubuf[0:pa - (ka - 1), :] = jnp.zeros((pa - (ka - 1), c), F32)
            ubuf[pa - (ka - 1):pa, :] = ha_ref[...]
            zbuf[0:pb - (kb - 1), :] = jnp.zeros((pb - (kb - 1), c), F32)
            zbuf[pb - (kb - 1):pb, :] = hb_ref[...]
        else:
            ubuf[0:pa, :] = jnp.zeros((pa, c), F32)
            zbuf[0:pb, :] = jnp.zeros((pb, c), F32)

    @pl.when(s > 0)
    def _():
        ubuf[0:pa, :] = ubuf[ts:ts + pa, :]
        zbuf[0:pb, :] = zbuf[ts:ts + pb, :]

    ubuf[pa:, :] = av_ref[...] * jax.nn.sigmoid(ag_ref[...])
    zbuf[pb:, :] = gc_ref[...] * z_ref[...]

    rc = min(CONV_ROW_CHUNK, ts)
    cc = min(CONV_COL_CHUNK, c)
    for r0 in range(0, ts, rc):
        for c0 in range(0, c, cc):
            acc = jnp.zeros((rc, cc), F32)
            for k in range(ka):
                start = r0 + k + pa - (ka - 1)
                acc = acc + dwa_ref[k:k + 1, c0:c0 + cc] * ubuf[start:start + rc, c0:c0 + cc]
            cbuf[r0:r0 + rc, c0:c0 + cc] = acc
    ya = cbuf[...] + ba_ref[...]
    mu = jnp.mean(ya, axis=-1, keepdims=True)
    yc = ya - mu
    ya = yc * lax.rsqrt(jnp.mean(yc * yc, axis=-1, keepdims=True) + EPS) * lng_ref[...] + lnb_ref[...]
    y_ref[:, 0:c] = _silu(ya).astype(y_ref.dtype)

    acc = jnp.zeros((ts, c), F32)
    for k in range(kb):
        start = k + pb - (kb - 1)
        acc = acc + dwb_ref[k:k + 1, :] * zbuf[start:start + ts, :]
    y_ref[:, c:2 * c] = (gb_ref[...] * acc).astype(y_ref.dtype)

    @pl.when(s == pl.num_programs(1) - 1)
    def _():
        ca_ref[...] = ubuf[ts + pa - (ka - 1):ts + pa, :]
        cb_ref[...] = zbuf[ts + pb - (kb - 1):ts + pb, :]


def _conv_mix(proj, hist_a, hist_b, dw_a, b_a, ln_g, ln_b, dw_b, li, n, ts, y_dtype):
    c = proj[0].shape[-1]
    s = proj[0].shape[0] // n
    ka, kb = dw_a.shape[1], dw_b.shape[1]
    pa = -(-(ka - 1) // 8) * 8
    pb = -(-(kb - 1) // 8) * 8
    has_hist = hist_a is not None
    tile = pl.BlockSpec((None, ts, c), lambda b, t: (b, t, 0))
    vec = pl.BlockSpec((None, 1, c), lambda b, t: (li, 0, 0))
    in_specs = [tile] * 5
    args = [p.reshape(n, s, c) for p in proj]
    if has_hist:
        in_specs += [pl.BlockSpec((None, None, ka - 1, c), lambda b, t: (li, b, 0, 0)),
                     pl.BlockSpec((None, None, kb - 1, c), lambda b, t: (li, b, 0, 0))]
        args += [hist_a, hist_b]
    in_specs += [pl.BlockSpec((None, ka, c), lambda b, t: (li, 0, 0)), vec, vec, vec,
                 pl.BlockSpec((None, kb, c), lambda b, t: (li, 0, 0))]
    nl = dw_a.shape[0]
    args += [dw_a, b_a.reshape(nl, 1, c), ln_g.reshape(nl, 1, c), ln_b.reshape(nl, 1, c), dw_b]
    y, ca, cb = pl.pallas_call(
        functools.partial(_conv_kernel, ka=ka, kb=kb, has_hist=has_hist),
        out_shape=[jax.ShapeDtypeStruct((n, s, 2 * c), y_dtype),
                   jax.ShapeDtypeStruct((n, ka - 1, c), F32),
                   jax.ShapeDtypeStruct((n, kb - 1, c), F32)],
        grid=(n, s // ts),
        in_specs=in_specs,
        out_specs=[pl.BlockSpec((None, ts, 2 * c), lambda b, t: (b, t, 0)),
                   pl.BlockSpec((None, ka - 1, c), lambda b, t: (b, 0, 0)),
                   pl.BlockSpec((None, kb - 1, c), lambda b, t: (b, 0, 0))],
        scratch_shapes=[pltpu.VMEM((ts + pa, c), F32), pltpu.VMEM((ts + pb, c), F32),
                        pltpu.VMEM((ts, c), F32)],
        compiler_params=_params(2),
        name="conv_mix",
    )(*args)
    return y.reshape(n * s, 2 * c), ca, cb


def _log_multiplicity(dist):
    cnt = jnp.zeros(dist.shape, jnp.int32)
    for window, dil in DIL_PATTERNS:
        assert dil & (dil - 1) == 0
        hit = (dist >= 0) & (dist <= window) & ((dist & (dil - 1)) == 0)
        cnt = cnt + hit.astype(jnp.int32)
    return jnp.where(cnt > 0, jnp.log(jnp.maximum(cnt, 1).astype(F32)), NEG)


def _attn_prompt_kernel(q_ref, k_ref, v_ref, slope_ref, o_ref,
                        kb_ref, vb_ref, tab_ref, rel_ref):
    qi = pl.program_id(2)
    tq, dh = q_ref.shape
    nd = tab_ref.shape[0]

    @pl.when((pl.program_id(0) == 0) & (pl.program_id(1) == 0) & (qi == 0))
    def _():
        rel = (lax.broadcasted_iota(jnp.int32, (tq, tq), 0)
               - lax.broadcasted_iota(jnp.int32, (tq, tq), 1))
        rel_ref[...] = rel.astype(F32)
        for delta in range(nd):
            tab_ref[delta] = _log_multiplicity(rel + delta * tq)

    @pl.when(qi == 0)
    def _():
        kb_ref[...] = k_ref[...].astype(BF16)
        vb_ref[...] = v_ref[...].astype(BF16)

    slope = slope_ref[...]
    q = (q_ref[...] * (dh ** -0.5)).astype(BF16)
    alibi = -slope * rel_ref[...]

    def body(kblk, carry):
        m, l, acc = carry
        delta = qi - kblk
        r0 = pl.multiple_of(kblk * tq, tq)
        s = _dot_nt(q, kb_ref[pl.ds(r0, tq), :])
        s = s + tab_ref[delta] + alibi - slope * (delta * tq).astype(F32)
        m_new = jnp.maximum(m, jnp.max(s, axis=-1, keepdims=True))
        alpha = jnp.exp(m - m_new)
        p = jnp.exp(s - m_new)
        l = alpha * l + jnp.sum(p, axis=-1, keepdims=True)
        acc = alpha * acc + _dot(p.astype(BF16), vb_ref[pl.ds(r0, tq), :])
        return m_new, l, acc

    init = (jnp.full((tq, 1), NEG, F32), jnp.zeros((tq, 1), F32), jnp.zeros((tq, dh), F32))
    _, l, acc = lax.fori_loop(0, qi + 1, body, init)
    o_ref[...] = (acc / l).astype(o_ref.dtype)


def _alibi_slopes(n_heads):
    return jnp.exp2(-8.0 * jnp.arange(1, n_heads + 1, dtype=F32) / n_heads)


def _attn_prompt(q, k, v, n, tq):
    m, d = q.shape
    s = m // n
    n_heads = d // HEAD_DIM
    nq = s // tq
    slopes = jnp.broadcast_to(_alibi_slopes(n_heads)[:, None, None], (n_heads, 1, tq))
    return pl.pallas_call(
        _attn_prompt_kernel,
        out_shape=jax.ShapeDtypeStruct((m, d), BF16),
        grid=(n, n_heads, nq),
        in_specs=[
            pl.BlockSpec((tq, HEAD_DIM), lambda b, h, i: (b * nq + i, h)),
            pl.BlockSpec((s, HEAD_DIM), lambda b, h, i: (b, h)),
            pl.BlockSpec((s, HEAD_DIM), lambda b, h, i: (b, h)),
            pl.BlockSpec((None, 1, tq), lambda b, h, i: (h, 0, 0)),
        ],
        out_specs=pl.BlockSpec((tq, HEAD_DIM), lambda b, h, i: (b * nq + i, h)),
        scratch_shapes=[pltpu.VMEM((s, HEAD_DIM), BF16), pltpu.VMEM((s, HEAD_DIM), BF16),
                        pltpu.VMEM((nq, tq, tq), F32), pltpu.VMEM((tq, tq), F32)],
        compiler_params=_params(3),
        name="attn_prompt",
    )(q, k, v, slopes)


SAMPLE_Q_ROWS = 8


def _attn_sample_kernel(q_ref, kc_ref, vc_ref, kn_ref, vn_ref, o_ref, m_ref, l_ref, acc_ref,
                        *, t_new, n_heads, n_cache_blocks):
    j = pl.program_id(1)
    tk = kc_ref.shape[0]
    wb = n_cache_blocks * tk
    rq = q_ref.shape[0]
    slopes = [2.0 ** (-8.0 * (h + 1) / n_heads) for h in range(n_heads)]

    @pl.when(j == 0)
    def _():
        m_ref[...] = jnp.full(m_ref.shape, NEG, F32)
        l_ref[...] = jnp.zeros(l_ref.shape, F32)
        acc_ref[...] = jnp.zeros(acc_ref.shape, F32)

    def attend(k_of_head, v_of_head, n_keys, first_row, n_valid_keys):
        t = lax.broadcasted_iota(jnp.int32, (rq, n_keys), 0)
        r = lax.broadcasted_iota(jnp.int32, (rq, n_keys), 1)
        dist = wb + t - (first_row + r)
        logc = jnp.where(r < n_valid_keys, _log_multiplicity(dist), NEG)
        distf = dist.astype(F32)
        for h in range(n_heads):
            cols = slice(h * HEAD_DIM, (h + 1) * HEAD_DIM)
            q = (q_ref[:, cols] * (HEAD_DIM ** -0.5)).astype(BF16)
            s = _dot_nt(q, k_of_head(cols).astype(BF16)) + logc - slopes[h] * distf
            m_old = m_ref[h]
            m_new = jnp.maximum(m_old, jnp.max(s, axis=-1, keepdims=True))
            alpha = jnp.exp(m_old - m_new)
            p = jnp.exp(s - m_new)
            l_ref[h] = alpha * l_ref[h] + jnp.sum(p, axis=-1, keepdims=True)
            acc_ref[h] = alpha * acc_ref[h] + _dot(p.astype(BF16), v_of_head(cols).astype(BF16))
            m_ref[h] = m_new

    @pl.when(j < n_cache_blocks)
    def _():
        attend(lambda cols: kc_ref[:, cols], lambda cols: vc_ref[:, cols], tk, j * tk, tk)

    @pl.when(j == n_cache_blocks)
    def _():
        attend(lambda cols: kn_ref[:, cols], lambda cols: vn_ref[:, cols], rq, wb, t_new)
        for h in range(n_heads):
            cols = slice(h * HEAD_DIM, (h + 1) * HEAD_DIM)
            o_ref[:, cols] = acc_ref[h] / l_ref[h]


def _attn_sample(q, k_new, v_new, cache_k, cache_v, li, nb, tk):
    m, d = q.shape
    t = m // nb
    wb = cache_k.shape[2]
    n_heads = d // HEAD_DIM
    rq = SAMPLE_Q_ROWS
    pad = lambda a: jnp.pad(a.reshape(nb, t, d), ((0, 0), (0, rq - t), (0, 0)))
    ncb = wb // tk
    new = pl.BlockSpec((None, rq, d), lambda b, j: (b, 0, 0))
    cache = pl.BlockSpec((None, None, tk, d), lambda b, j: (li, b, jnp.minimum(j, ncb - 1), 0))
    o = pl.pallas_call(
        functools.partial(_attn_sample_kernel, t_new=t, n_heads=n_heads, n_cache_blocks=ncb),
        out_shape=jax.ShapeDtypeStruct((nb, rq, d), F32),
        grid=(nb, ncb + 1),
        in_specs=[new, cache, cache, new, new],
        out_specs=new,
        scratch_shapes=[pltpu.VMEM((n_heads, rq, 1), F32), pltpu.VMEM((n_heads, rq, 1), F32),
                        pltpu.VMEM((n_heads, rq, HEAD_DIM), F32)],
        compiler_params=_params(2),
        name="attn_sample",
    )(pad(q), cache_k.reshape(cache_k.shape[:3] + (d,)), cache_v.reshape(cache_v.shape[:3] + (d,)),
      pad(k_new), pad(v_new))
    return o[:, :t].reshape(m, d)


def _trunk(x3, mod_table, per_row, w, caches, tiles):
    tm, tf, tp = tiles["ffn_rows"], tiles["ffn_hidden"], tiles["proj_rows"]
    n, s, d = x3.shape
    x = x3.reshape(n * s, d)
    depth = w["norm_g4"].shape[0]
    new_a, new_b, new_k, new_v = [], [], [], []
    for layer in range(depth):
        li = layer // 2
        mods = [_Mod(mod_table, per_row, layer, sub, w["norm_g4"]) for sub in range(3)]
        x = _ffn(x, mods[0], w["wg"], w["wu"], w["wd"], 0, s, tm, tf)
        if layer % 2 == 0:
            proj = _inproj(x, mods[1], w["conv_w_in"], li, 5, s, tp)
            hist_a, hist_b = caches[0], caches[1]
            y, ca, cb = _conv_mix(proj, hist_a, hist_b, w["conv_dw_a"], w["conv_b_a"],
                                  w["conv_ln_g"], w["conv_ln_b"], w["conv_dw_b"], li, n,
                                  tiles["conv_rows"],
                                  F32 if per_row else BF16)
            new_a.append(ca)
            new_b.append(cb)
            x = _outproj(y, x, mods[1], w["conv_w_out"], li, s, tp)
        else:
            q, k, v = _inproj(x, mods[1], w["attn_w_qkv"], li, 3, s, tp)
            if per_row:
                o = _attn_sample(q, k, v, caches[2], caches[3], li, n, tiles["cache_rows"])
            else:
                o = _attn_prompt(q, k, v, n, tiles["attn_rows"])
            keep = s if per_row else min(DIL_PATTERNS[-1][0], s)
            n_heads = d // HEAD_DIM
            new_k.append(k.reshape(n, s, n_heads, HEAD_DIM)[:, s - keep:])
            new_v.append(v.reshape(n, s, n_heads, HEAD_DIM)[:, s - keep:])
            x = _outproj(o, x, mods[1], w["attn_w_o"], li, s, tp)
        x = _ffn(x, mods[2], w["wg"], w["wu"], w["wd"], 1, s, tm, tf)
    return x.reshape(n, s, d), jnp.stack(new_a), jnp.stack(new_b), jnp.stack(new_k), jnp.stack(new_v)


def _tile(total, want):
    t = min(total, want)
    while total % t:
        t //= 2
    return t


def kernel(x_prompt, x_sample, cache_conv_a, cache_conv_b, cache_k, cache_v, c_prompt, c_sample,
           w_mod, b_mod, norm_g, ffn_w_gate, ffn_w_up, ffn_w_down, conv_w_in, conv_dw_a, conv_b_a,
           conv_ln_g, conv_ln_b, conv_dw_b, conv_w_out, attn_w_qkv, attn_w_o):
    n, s, d = x_prompt.shape
    nb, t, _ = x_sample.shape
    depth = w_mod.shape[0]

    mod = _mod_table(jnp.concatenate([c_prompt, c_sample], axis=0), w_mod, b_mod,
                     tn=_tile(w_mod.shape[-1], 1024))
    mod_p = mod[:, :n].reshape(depth, n, N_MOD, 1, d)
    mod_s = jnp.repeat(mod[:, n:].reshape(depth, nb, N_MOD, d), t, axis=1).transpose(0, 2, 1, 3)

    w = dict(
        norm_g4=norm_g.reshape(depth, 6, 1, d),
        wg=ffn_w_gate.astype(BF16), wu=ffn_w_up.astype(BF16), wd=ffn_w_down.astype(BF16),
        conv_w_in=conv_w_in.astype(BF16), conv_w_out=conv_w_out.astype(BF16),
        attn_w_qkv=attn_w_qkv.astype(BF16), attn_w_o=attn_w_o.astype(BF16),
        conv_dw_a=conv_dw_a, conv_b_a=conv_b_a, conv_ln_g=conv_ln_g, conv_ln_b=conv_ln_b,
        conv_dw_b=conv_dw_b,
    )
    tf = _tile(ffn_w_gate.shape[-1], 512)

    tiles_prompt = dict(ffn_rows=_tile(s, 512), ffn_hidden=tf, proj_rows=_tile(s, 256),
                        conv_rows=_tile(s, 256), attn_rows=_tile(s, 256))
    tiles_sample = dict(ffn_rows=nb * t, ffn_hidden=tf, proj_rows=nb * t, conv_rows=t,
                        cache_rows=_tile(cache_k.shape[2], 512))
    yp, ap, bp, kp, vp = _trunk(x_prompt, mod_p, False, w, (None, None, None, None), tiles_prompt)
    ys, as_, bs, ks, vs = _trunk(x_sample, mod_s, True, w,
                                 (cache_conv_a, cache_conv_b, cache_k, cache_v), tiles_sample)
    return (yp, ys, ap, as_, bp, bs, kp, vp, ks, vs)
```

```python
import functools

import numpy as np
import jax
import jax.numpy as jnp
from jax import lax
from jax.experimental import pallas as pl
from jax.experimental.pallas import tpu as pltpu

F32 = jnp.float32
BF16 = jnp.bfloat16

HEAD_DIM = 128
DIL_PATTERNS = ((128, 1), (512, 4), (2048, 16))
N_MOD = 9
EPS = 1e-6
NEG = -1e30

V7X_VMEM_BYTES = 64 * 1024 * 1024
VMEM_LIMIT_BYTES = V7X_VMEM_BYTES - 8 * 1024 * 1024
ROW_CHUNK = 16


def _params(n_axes):
    return pltpu.CompilerParams(
        dimension_semantics=("arbitrary",) * n_axes,
        vmem_limit_bytes=VMEM_LIMIT_BYTES,
    )


def _silu(x):
    return x * jax.nn.sigmoid(x)


def _dot(a, b):
    return jnp.dot(a, b, preferred_element_type=F32)


def _dot_nt(a, b):
    return lax.dot_general(a, b, (((1,), (1,)), ((), ())), preferred_element_type=F32)


def _inv_rms(x):
    return lax.rsqrt(jnp.mean(jnp.square(x), axis=-1, keepdims=True) + EPS)


def _mod_rows(ref, rows):
    return ref[rows, :] if ref.shape[0] > 1 else ref[...]


def _modulate_rows(x_ref, gin_ref, sc_ref, sh_ref, h_ref):
    per_row = sc_ref.shape[0] > 1
    gs = None if per_row else gin_ref[...] * (1.0 + sc_ref[...])
    for r0 in range(0, x_ref.shape[0], ROW_CHUNK):
        rows = slice(r0, r0 + ROW_CHUNK)
        if per_row:
            gs = gin_ref[...] * (1.0 + sc_ref[rows, :])
        h = x_ref[rows, :] * _inv_rms(x_ref[rows, :]) * gs + _mod_rows(sh_ref, rows)
        h_ref[rows, :] = h.astype(h_ref.dtype)


def _gated_residual_rows(x_ref, gt_ref, gout_ref, o_ref, weight):
    per_row = gt_ref.shape[0] > 1
    gs = None if per_row else weight * gt_ref[...] * gout_ref[...]
    for r0 in range(0, x_ref.shape[0], ROW_CHUNK):
        rows = slice(r0, r0 + ROW_CHUNK)
        if per_row:
            gs = weight * gt_ref[rows, :] * gout_ref[...]
        o_ref[rows, :] = x_ref[rows, :] + o_ref[rows, :] * _inv_rms(o_ref[rows, :]) * gs


def _mod_kernel(c_ref, w_ref, b_ref, o_ref):
    a = _silu(c_ref[...]).astype(BF16)
    o_ref[...] = _dot(a, w_ref[...].astype(BF16)) + b_ref[...]


def _mod_table(c_all, w_mod, b_mod, tn):
    n_layers, d, n9 = w_mod.shape
    r = c_all.shape[0]
    return pl.pallas_call(
        _mod_kernel,
        out_shape=jax.ShapeDtypeStruct((n_layers, r, n9), F32),
        grid=(n_layers, n9 // tn),
        in_specs=[
            pl.BlockSpec((r, d), lambda l, j: (0, 0)),
            pl.BlockSpec((None, d, tn), lambda l, j: (l, 0, j)),
            pl.BlockSpec((None, 1, tn), lambda l, j: (l, 0, j)),
        ],
        out_specs=pl.BlockSpec((None, r, tn), lambda l, j: (l, 0, j)),
        compiler_params=_params(2),
        name="mod_table",
    )(c_all, w_mod, b_mod.reshape(n_layers, 1, n9))


class _Mod:
    def __init__(self, table, per_row, layer, sub, norm_g4):
        self.table = table
        self.per_row = per_row
        self.layer = layer
        self.sub = sub
        self.norm_g4 = norm_g4

    def spec(self, which, tm, rows_per_group):
        l, idx = self.layer, 3 * self.sub + which
        d = self.table.shape[-1]
        if self.per_row:
            return pl.BlockSpec((None, None, tm, d), lambda i, *_: (l, idx, i, 0))
        return pl.BlockSpec((None, None, None, 1, d),
                            lambda i, *_: (l, (i * tm) // rows_per_group, idx, 0, 0))

    def gain_spec(self, which):
        l, idx = self.layer, 2 * self.sub + which
        d = self.norm_g4.shape[-1]
        return pl.BlockSpec((None, None, 1, d), lambda i, *_: (l, idx, 0, 0))


def _ffn_kernel(x_ref, sh_ref, sc_ref, gt_ref, gin_ref, gout_ref, wg_ref, wu_ref, wd_ref,
                o_ref, h_ref, a_ref):
    j = pl.program_id(1)

    @pl.when(j == 0)
    def _():
        _modulate_rows(x_ref, gin_ref, sc_ref, sh_ref, h_ref)
        o_ref[...] = jnp.zeros(o_ref.shape, F32)

    g = _dot(h_ref[...], wg_ref[...])
    u = _dot(h_ref[...], wu_ref[...])
    a_ref[...] = (_silu(g) * u).astype(BF16)
    o_ref[...] += _dot(a_ref[...], wd_ref[...])

    @pl.when(j == pl.num_programs(1) - 1)
    def _():
        _gated_residual_rows(x_ref, gt_ref, gout_ref, o_ref, 0.5)


def _ffn(x, mod, wg, wu, wd, half, rows_per_group, tm, tf):
    m, d = x.shape
    f = wg.shape[-1]
    l = mod.layer
    row = pl.BlockSpec((tm, d), lambda i, j: (i, 0))
    return pl.pallas_call(
        _ffn_kernel,
        out_shape=jax.ShapeDtypeStruct((m, d), F32),
        grid=(m // tm, f // tf),
        in_specs=[
            row,
            mod.spec(0, tm, rows_per_group), mod.spec(1, tm, rows_per_group),
            mod.spec(2, tm, rows_per_group),
            mod.gain_spec(0), mod.gain_spec(1),
            pl.BlockSpec((None, None, d, tf), lambda i, j: (l, half, 0, j)),
            pl.BlockSpec((None, None, d, tf), lambda i, j: (l, half, 0, j)),
            pl.BlockSpec((None, None, tf, d), lambda i, j: (l, half, j, 0)),
        ],
        out_specs=row,
        scratch_shapes=[pltpu.VMEM((tm, d), BF16), pltpu.VMEM((tm, tf), BF16)],
        compiler_params=_params(2),
        name="ffn",
    )(x, mod.table, mod.table, mod.table, mod.norm_g4, mod.norm_g4, wg, wu, wd)


def _inproj_kernel(x_ref, sh_ref, sc_ref, gin_ref, w_ref, *rest):
    o_refs, h_ref = rest[:-1], rest[-1]
    _modulate_rows(x_ref, gin_ref, sc_ref, sh_ref, h_ref)
    for k, o_ref in enumerate(o_refs):
        tn = o_ref.shape[-1]
        o_ref[...] = _dot(h_ref[...], w_ref[:, k * tn:(k + 1) * tn])


def _inproj(x, mod, w, li, n_out, rows_per_group, tm):
    m, d = x.shape
    n_cols = w.shape[-1]
    tn = n_cols // n_out
    return pl.pallas_call(
        _inproj_kernel,
        out_shape=[jax.ShapeDtypeStruct((m, tn), F32)] * n_out,
        grid=(m // tm,),
        in_specs=[
            pl.BlockSpec((tm, d), lambda i: (i, 0)),
            mod.spec(0, tm, rows_per_group), mod.spec(1, tm, rows_per_group),
            mod.gain_spec(0),
            pl.BlockSpec((None, d, n_cols), lambda i: (li, 0, 0), pipeline_mode=pl.Buffered(1)),
        ],
        out_specs=[pl.BlockSpec((tm, tn), lambda i: (i, 0))] * n_out,
        scratch_shapes=[pltpu.VMEM((tm, d), BF16)],
        compiler_params=_params(1),
        name="inproj",
    )(x, mod.table, mod.table, mod.norm_g4, w)


def _outproj_kernel(y_ref, x_ref, gt_ref, gout_ref, w_ref, o_ref):
    o_ref[...] = _dot(y_ref[...].astype(BF16), w_ref[...])
    _gated_residual_rows(x_ref, gt_ref, gout_ref, o_ref, 1.0)


def _outproj(y, x, mod, w, li, rows_per_group, tm):
    m, d = x.shape
    k = y.shape[-1]
    return pl.pallas_call(
        _outproj_kernel,
        out_shape=jax.ShapeDtypeStruct((m, d), F32),
        grid=(m // tm,),
        in_specs=[
            pl.BlockSpec((tm, k), lambda i: (i, 0)),
            pl.BlockSpec((tm, d), lambda i: (i, 0)),
            mod.spec(2, tm, rows_per_group),
            mod.gain_spec(1),
            pl.BlockSpec((None, k, d), lambda i: (li, 0, 0), pipeline_mode=pl.Buffered(1)),
        ],
        out_specs=pl.BlockSpec((tm, d), lambda i: (i, 0)),
        compiler_params=_params(1),
        name="outproj",
    )(y, x, mod.table, mod.norm_g4, w)


CONV_ROW_CHUNK = 32
CONV_COL_CHUNK = 512


def _conv_kernel(*refs, ka, kb, has_hist):
    if has_hist:
        (av_ref, ag_ref, gb_ref, gc_ref, z_ref, ha_ref, hb_ref,
         dwa_ref, ba_ref, lng_ref, lnb_ref, dwb_ref,
         y_ref, ca_ref, cb_ref, ubuf, zbuf, cbuf) = refs
    else:
        (av_ref, ag_ref, gb_ref, gc_ref, z_ref,
         dwa_ref, ba_ref, lng_ref, lnb_ref, dwb_ref,
         y_ref, ca_ref, cb_ref, ubuf, zbuf, cbuf) = refs
    s = pl.program_id(1)
    ts, c = av_ref.shape
    pa = ubuf.shape[0] - ts
    pb = zbuf.shape[0] - ts

    @pl.when(s == 0)
    def _():
        if has_hist:
            ubuf[0:pa - (ka - 1), :] = jnp.zeros((pa - (ka - 1), c), F32)
            ubuf[pa - (ka - 1):pa, :] = ha_ref[...]
            zbuf[0:pb - (kb - 1), :] = jnp.zeros((pb - (kb - 1), c), F32)
            zbuf[pb - (kb - 1):pb, :] = hb_ref[...]
        else:
            ubuf[0:pa, :] = jnp.zeros((pa, c), F32)
            zbuf[0:pb, :] = jnp.zeros((pb, c), F32)

    @pl.when(s > 0)
    def _():
        ubuf[0:pa, :] = ubuf[ts:ts + pa, :]
        zbuf[0:pb, :] = zbuf[ts:ts + pb, :]

    ubuf[pa:, :] = av_ref[...] * jax.nn.sigmoid(ag_ref[...])
    zbuf[pb:, :] = gc_ref[...] * z_ref[...]

    rc = min(CONV_ROW_CHUNK, ts)
    cc = min(CONV_COL_CHUNK, c)
    for r0 in range(0, ts, rc):
        for c0 in range(0, c, cc):
            acc = jnp.zeros((rc, cc), F32)
            for k in range(ka):
                start = r0 + k + pa - (ka - 1)
                acc = acc + dwa_ref[k:k + 1, c0:c0 + cc] * ubuf[start:start + rc, c0:c0 + cc]
            cbuf[r0:r0 + rc, c0:c0 + cc] = acc
    ya = cbuf[...] + ba_ref[...]
    mu = jnp.mean(ya, axis=-1, keepdims=True)
    yc = ya - mu
    ya = yc * lax.rsqrt(jnp.mean(yc * yc, axis=-1, keepdims=True) + EPS) * lng_ref[...] + lnb_ref[...]
    y_ref[:, 0:c] = _silu(ya).astype(y_ref.dtype)

    acc = jnp.zeros((ts, c), F32)
    for k in range(kb):
        start = k + pb - (kb - 1)
        acc = acc + dwb_ref[k:k + 1, :] * zbuf[start:start + ts, :]
    y_ref[:, c:2 * c] = (gb_ref[...] * acc).astype(y_ref.dtype)

    @pl.when(s == pl.num_programs(1) - 1)
    def _():
        ca_ref[...] = ubuf[ts + pa - (ka - 1):ts + pa, :]
        cb_ref[...] = zbuf[ts + pb - (kb - 1):ts + pb, :]


def _conv_mix(proj, hist_a, hist_b, dw_a, b_a, ln_g, ln_b, dw_b, li, n, ts, y_dtype):
    c = proj[0].shape[-1]
    s = proj[0].shape[0] // n
    ka, kb = dw_a.shape[1], dw_b.shape[1]
    pa = -(-(ka - 1) // 8) * 8
    pb = -(-(kb - 1) // 8) * 8
    has_hist = hist_a is not None
    tile = pl.BlockSpec((None, ts, c), lambda b, t: (b, t, 0))
    vec = pl.BlockSpec((None, 1, c), lambda b, t: (li, 0, 0))
    in_specs = [tile] * 5
    args = [p.reshape(n, s, c) for p in proj]
    if has_hist:
        in_specs += [pl.BlockSpec((None, None, ka - 1, c), lambda b, t: (li, b, 0, 0)),
                     pl.BlockSpec((None, None, kb - 1, c), lambda b, t: (li, b, 0, 0))]
        args += [hist_a, hist_b]
    in_specs += [pl.BlockSpec((None, ka, c), lambda b, t: (li, 0, 0)), vec, vec, vec,
                 pl.BlockSpec((None, kb, c), lambda b, t: (li, 0, 0))]
    nl = dw_a.shape[0]
    args += [dw_a, b_a.reshape(nl, 1, c), ln_g.reshape(nl, 1, c), ln_b.reshape(nl, 1, c), dw_b]
    y, ca, cb = pl.pallas_call(
        functools.partial(_conv_kernel, ka=ka, kb=kb, has_hist=has_hist),
        out_shape=[jax.ShapeDtypeStruct((n, s, 2 * c), y_dtype),
                   jax.ShapeDtypeStruct((n, ka - 1, c), F32),
                   jax.ShapeDtypeStruct((n, kb - 1, c), F32)],
        grid=(n, s // ts),
        in_specs=in_specs,
        out_specs=[pl.BlockSpec((None, ts, 2 * c), lambda b, t: (b, t, 0)),
                   pl.BlockSpec((None, ka - 1, c), lambda b, t: (b, 0, 0)),
                   pl.BlockSpec((None, kb - 1, c), lambda b, t: (b, 0, 0))],
        scratch_shapes=[pltpu.VMEM((ts + pa, c), F32), pltpu.VMEM((ts + pb, c), F32),
                        pltpu.VMEM((ts, c), F32)],
        compiler_params=_params(2),
        name="conv_mix",
    )(*args)
    return y.reshape(n * s, 2 * c), ca, cb


def _multiplicity(dist, xp):
    cnt = 0
    for window, dil in DIL_PATTERNS:
        assert dil & (dil - 1) == 0
        hit = (dist >= 0) & (dist <= window) & ((dist & (dil - 1)) == 0)
        cnt = cnt + hit.astype(xp.int32)
    return cnt


def _log_multiplicity(dist):
    cnt = _multiplicity(dist, jnp)
    return jnp.where(cnt > 0, jnp.log(jnp.maximum(cnt, 1).astype(F32)), NEG)


def _attn_prompt_kernel(q_ref, k_ref, v_ref, slope_ref, o_ref,
                        qb_ref, kb_ref, vb_ref, s_ref, p_ref, bias_ref, logc_ref, dist_ref, *, tq):
    s_len, dh = q_ref.shape
    nq = s_len // tq

    @pl.when((pl.program_id(0) == 0) & (pl.program_id(1) == 0))
    def _():
        dist = (lax.broadcasted_iota(jnp.int32, (tq, s_len), 0) + (s_len - tq)
                - lax.broadcasted_iota(jnp.int32, (tq, s_len), 1))
        logc_ref[...] = _log_multiplicity(dist)
        dist_ref[...] = dist.astype(F32)
        vb_ref[:, dh:] = jnp.ones((s_len, dh), BF16)

    bias_ref[...] = logc_ref[...] - slope_ref[...] * dist_ref[...]
    qb_ref[...] = (q_ref[...] * (dh ** -0.5)).astype(BF16)
    kb_ref[...] = k_ref[...].astype(BF16)
    vb_ref[:, :dh] = v_ref[...].astype(BF16)

    for i in range(nq):
        slot = i % 2
        c0 = (nq - 1 - i) * tq
        q = qb_ref[i * tq:(i + 1) * tq, :]
        m = jnp.full((tq, 1), NEG, F32)
        for j in range(i + 1):
            keys = slice(j * tq, (j + 1) * tq)
            s = _dot_nt(q, kb_ref[keys, :]) + bias_ref[:, c0 + j * tq:c0 + (j + 1) * tq]
            s_ref[slot, :, keys] = s
            m = jnp.maximum(m, jnp.max(s, axis=-1, keepdims=True))
        for j in range(i + 1):
            keys = slice(j * tq, (j + 1) * tq)
            p_ref[slot, :, keys] = jnp.exp(s_ref[slot, :, keys] - m).astype(BF16)
        n_keys = (i + 1) * tq
        ol = _dot(p_ref[slot, :, :n_keys], vb_ref[:n_keys, :])
        o_ref[i * tq:(i + 1) * tq, :] = (ol[:, :dh] / ol[:, dh:]).astype(o_ref.dtype)


def _alibi_slopes(n_heads):
    return jnp.exp2(-8.0 * jnp.arange(1, n_heads + 1, dtype=F32) / n_heads)


def _attn_prompt(q, k, v, n, tq):
    m, d = q.shape
    s = m // n
    n_heads = d // HEAD_DIM
    slopes = jnp.broadcast_to(_alibi_slopes(n_heads)[:, None, None], (n_heads, 1, s))
    head = pl.BlockSpec((s, HEAD_DIM), lambda b, h: (b, h))
    return pl.pallas_call(
        functools.partial(_attn_prompt_kernel, tq=tq),
        out_shape=jax.ShapeDtypeStruct((m, d), BF16),
        grid=(n, n_heads),
        in_specs=[head, head, head, pl.BlockSpec((None, 1, s), lambda b, h: (h, 0, 0))],
        out_specs=head,
        scratch_shapes=[pltpu.VMEM((s, HEAD_DIM), BF16), pltpu.VMEM((s, HEAD_DIM), BF16),
                        pltpu.VMEM((s, 2 * HEAD_DIM), BF16),
                        pltpu.VMEM((2, tq, s), F32), pltpu.VMEM((2, tq, s), BF16),
                        pltpu.VMEM((tq, s), F32), pltpu.VMEM((tq, s), F32), pltpu.VMEM((tq, s), F32)],
        compiler_params=_params(2),
        name="attn_prompt",
    )(q, k, v, slopes)


def _sample_regions(wb, t_new):
    dil_max = max(dil for _, dil in DIL_PATTERNS)
    reach = max(window for window, dil in DIL_PATTERNS if dil < dil_max)
    near = -(-reach // dil_max) * dil_max
    assert wb % dil_max == 0 and wb > near and t_new <= dil_max
    r = np.arange(wb - near)
    skipped = r[(r % dil_max) >= t_new]
    for t in range(t_new):
        assert not _multiplicity(wb + t - skipped, np).any()
    return dil_max, near


def _shift(x, n):
    assert n & (n - 1) == 0
    return x >> (n.bit_length() - 1), x & (n - 1)


def _attn_sample_kernel(q_ref, kfar_ref, vfar_ref, knear_ref, vnear_ref, knew_ref, vnew_ref,
                        o_ref, bfar_ref, bnear_ref, bnew_ref, *, t_new, n_heads, wb, dil_max):
    n_far = kfar_ref.shape[0] * kfar_ref.shape[1] * kfar_ref.shape[2]
    n_near = knear_ref.shape[0]
    dh = q_ref.shape[-1]

    @pl.when(pl.program_id(0) == 0)
    def _():
        def table(ref, row_of_slot):
            i = lax.broadcasted_iota(jnp.int32, ref.shape, 0)
            c = lax.broadcasted_iota(jnp.int32, ref.shape, 1)
            h, t = _shift(i, t_new)
            slot, hk = _shift(c, n_heads)
            dist = wb + t - row_of_slot(slot)
            slope = jnp.exp2(-8.0 * (h + 1).astype(F32) / n_heads)
            ref[...] = jnp.where(h == hk, _log_multiplicity(dist) - slope * dist.astype(F32), NEG)

        def far_row(slot):
            g, p = _shift(slot, t_new)
            return g * dil_max + p

        table(bfar_ref, far_row)
        table(bnear_ref, lambda slot: slot + (wb - n_near // n_heads))
        table(bnew_ref, lambda slot: slot + wb)

    q = (q_ref[...] * (dh ** -0.5)).astype(BF16)
    kfar = kfar_ref[...].reshape(n_far, dh).astype(BF16)
    s_far = _dot_nt(q, kfar) + bfar_ref[...]
    s_near = _dot_nt(q, knear_ref[...].astype(BF16)) + bnear_ref[...]
    s_new = _dot_nt(q, knew_ref[...].astype(BF16)) + bnew_ref[...]
    m = jnp.maximum(jnp.maximum(jnp.max(s_far, axis=-1, keepdims=True),
                                jnp.max(s_near, axis=-1, keepdims=True)),
                    jnp.max(s_new, axis=-1, keepdims=True))
    p_far = jnp.exp(s_far - m)
    p_near = jnp.exp(s_near - m)
    p_new = jnp.exp(s_new - m)
    l = (jnp.sum(p_far, axis=-1, keepdims=True) + jnp.sum(p_near, axis=-1, keepdims=True)
         + jnp.sum(p_new, axis=-1, keepdims=True))
    vfar = vfar_ref[...].reshape(n_far, dh).astype(BF16)
    o = (_dot(p_far.astype(BF16), vfar) + _dot(p_near.astype(BF16), vnear_ref[...].astype(BF16))
         + _dot(p_new.astype(BF16), vnew_ref[...].astype(BF16)))
    o_ref[...] = o / l


def _attn_sample(q, k_new, v_new, cache_k, cache_v, li, nb):
    m, d = q.shape
    t = m // nb
    n_layers, _, wb, n_heads, dh = cache_k.shape
    dil_max, near = _sample_regions(wb, t)
    n_groups = (wb - near) // dil_max
    by_head = lambda a: a.reshape(nb, t, n_heads, dh).transpose(0, 2, 1, 3).reshape(nb, n_heads * t, dh)
    rows = lambda a: a.reshape(nb, t * n_heads, dh)
    far = lambda c: c.reshape(n_layers, nb, wb // dil_max, dil_max, n_heads, dh)
    flat = lambda c: c.reshape(n_layers, nb, wb * n_heads, dh)
    qspec = pl.BlockSpec((None, n_heads * t, dh), lambda b: (b, 0, 0))
    far_spec = pl.BlockSpec((None, None, n_groups, t, n_heads, dh), lambda b: (li, b, 0, 0, 0, 0))
    near_spec = pl.BlockSpec((None, None, near * n_heads, dh), lambda b: (li, b, wb // near - 1, 0))
    assert wb % near == 0
    o = pl.pallas_call(
        functools.partial(_attn_sample_kernel, t_new=t, n_heads=n_heads, wb=wb, dil_max=dil_max),
        out_shape=jax.ShapeDtypeStruct((nb, n_heads * t, dh), F32),
        grid=(nb,),
        in_specs=[qspec, far_spec, far_spec, near_spec, near_spec, qspec, qspec],
        out_specs=qspec,
        scratch_shapes=[pltpu.VMEM((n_heads * t, n_groups * t * n_heads), F32),
                        pltpu.VMEM((n_heads * t, near * n_heads), F32),
                        pltpu.VMEM((n_heads * t, t * n_heads), F32)],
        compiler_params=_params(1),
        name="attn_sample",
    )(by_head(q), far(cache_k), far(cache_v), flat(cache_k), flat(cache_v), rows(k_new), rows(v_new))
    return o.reshape(nb, n_heads, t, dh).transpose(0, 2, 1, 3).reshape(m, d)


def _trunk(x3, mod_table, per_row, w, caches, tiles):
    tm, tf, tp = tiles["ffn_rows"], tiles["ffn_hidden"], tiles["proj_rows"]
    n, s, d = x3.shape
    x = x3.reshape(n * s, d)
    depth = w["norm_g4"].shape[0]
    new_a, new_b, new_k, new_v = [], [], [], []
    for layer in range(depth):
        li = layer // 2
        mods = [_Mod(mod_table, per_row, layer, sub, w["norm_g4"]) for sub in range(3)]
        x = _ffn(x, mods[0], w["wg"], w["wu"], w["wd"], 0, s, tm, tf)
        if layer % 2 == 0:
            proj = _inproj(x, mods[1], w["conv_w_in"], li, 5, s, tp)
            hist_a, hist_b = caches[0], caches[1]
            y, ca, cb = _conv_mix(proj, hist_a, hist_b, w["conv_dw_a"], w["conv_b_a"],
                                  w["conv_ln_g"], w["conv_ln_b"], w["conv_dw_b"], li, n,
                                  tiles["conv_rows"],
                                  F32 if per_row else BF16)
            new_a.append(ca)
            new_b.append(cb)
            x = _outproj(y, x, mods[1], w["conv_w_out"], li, s, tp)
        else:
            q, k, v = _inproj(x, mods[1], w["attn_w_qkv"], li, 3, s, tp)
            if per_row:
                o = _attn_sample(q, k, v, caches[2], caches[3], li, n)
            else:
                o = _attn_prompt(q, k, v, n, tiles["attn_rows"])
            keep = s if per_row else min(DIL_PATTERNS[-1][0], s)
            n_heads = d // HEAD_DIM
            new_k.append(k.reshape(n, s, n_heads, HEAD_DIM)[:, s - keep:])
            new_v.append(v.reshape(n, s, n_heads, HEAD_DIM)[:, s - keep:])
            x = _outproj(o, x, mods[1], w["attn_w_o"], li, s, tp)
        x = _ffn(x, mods[2], w["wg"], w["wu"], w["wd"], 1, s, tm, tf)
    return x.reshape(n, s, d), jnp.stack(new_a), jnp.stack(new_b), jnp.stack(new_k), jnp.stack(new_v)


def _tile(total, want):
    t = min(total, want)
    while total % t:
        t //= 2
    return t


def kernel(x_prompt, x_sample, cache_conv_a, cache_conv_b, cache_k, cache_v, c_prompt, c_sample,
           w_mod, b_mod, norm_g, ffn_w_gate, ffn_w_up, ffn_w_down, conv_w_in, conv_dw_a, conv_b_a,
           conv_ln_g, conv_ln_b, conv_dw_b, conv_w_out, attn_w_qkv, attn_w_o):
    n, s, d = x_prompt.shape
    nb, t, _ = x_sample.shape
    depth = w_mod.shape[0]

    mod = _mod_table(jnp.concatenate([c_prompt, c_sample], axis=0), w_mod, b_mod,
                     tn=_tile(w_mod.shape[-1], 1024))
    mod_p = mod[:, :n].reshape(depth, n, N_MOD, 1, d)
    mod_s = jnp.repeat(mod[:, n:].reshape(depth, nb, N_MOD, d), t, axis=1).transpose(0, 2, 1, 3)

    w = dict(
        norm_g4=norm_g.reshape(depth, 6, 1, d),
        wg=ffn_w_gate.astype(BF16), wu=ffn_w_up.astype(BF16), wd=ffn_w_down.astype(BF16),
        conv_w_in=conv_w_in.astype(BF16), conv_w_out=conv_w_out.astype(BF16),
        attn_w_qkv=attn_w_qkv.astype(BF16), attn_w_o=attn_w_o.astype(BF16),
        conv_dw_a=conv_dw_a, conv_b_a=conv_b_a, conv_ln_g=conv_ln_g, conv_ln_b=conv_ln_b,
        conv_dw_b=conv_dw_b,
    )
    tf = _tile(ffn_w_gate.shape[-1], 512)
    tiles_prompt = dict(ffn_rows=_tile(s, 1024), ffn_hidden=tf, proj_rows=_tile(s, 256),
                        conv_rows=_tile(s, 256), attn_rows=_tile(s, 256))
    tiles_sample = dict(ffn_rows=nb * t, ffn_hidden=tf, proj_rows=nb * t, conv_rows=t)
    yp, ap, bp, kp, vp = _trunk(x_prompt, mod_p, False, w, (None, None, None, None), tiles_prompt)
    ys, as_, bs, ks, vs = _trunk(x_sample, mod_s, True, w,
                                 (cache_conv_a, cache_conv_b, cache_k, cache_v), tiles_sample)
    return (yp, ys, ap, as_, bp, bs, kp, vp, ks, vs)
```

```python
import functools

import numpy as np
import jax
import jax.numpy as jnp
from jax import lax
from jax.experimental import pallas as pl
from jax.experimental.pallas import tpu as pltpu

F32 = jnp.float32
BF16 = jnp.bfloat16

HEAD_DIM = 128
DIL_PATTERNS = ((128, 1), (512, 4), (2048, 16))
N_MOD = 9
EPS = 1e-6
NEG = -1e30
LOG2E = 1.4426950408889634

V7X_VMEM_BYTES = 64 * 1024 * 1024
VMEM_LIMIT_BYTES = V7X_VMEM_BYTES - 8 * 1024 * 1024
ROW_CHUNK = 16


def _params(n_axes):
    return pltpu.CompilerParams(
        dimension_semantics=("arbitrary",) * n_axes,
        vmem_limit_bytes=VMEM_LIMIT_BYTES,
    )


def _silu(x):
    return x * jax.nn.sigmoid(x)


def _dot(a, b):
    return jnp.dot(a, b, preferred_element_type=F32)


def _dot_nt(a, b):
    return lax.dot_general(a, b, (((1,), (1,)), ((), ())), preferred_element_type=F32)


def _inv_rms(x):
    return lax.rsqrt(jnp.mean(jnp.square(x), axis=-1, keepdims=True) + EPS)


def _mod_rows(ref, rows):
    return ref[rows, :] if ref.shape[0] > 1 else ref[...]


def _modulate_rows(x_ref, gin_ref, sc_ref, sh_ref, h_ref):
    per_row = sc_ref.shape[0] > 1
    gs = None if per_row else gin_ref[...] * (1.0 + sc_ref[...])
    for r0 in range(0, x_ref.shape[0], ROW_CHUNK):
        rows = slice(r0, r0 + ROW_CHUNK)
        if per_row:
            gs = gin_ref[...] * (1.0 + sc_ref[rows, :])
        h = x_ref[rows, :] * _inv_rms(x_ref[rows, :]) * gs + _mod_rows(sh_ref, rows)
        h_ref[rows, :] = h.astype(h_ref.dtype)


def _gated_residual_rows(x_ref, gt_ref, gout_ref, o_ref, weight):
    per_row = gt_ref.shape[0] > 1
    gs = None if per_row else weight * gt_ref[...] * gout_ref[...]
    for r0 in range(0, x_ref.shape[0], ROW_CHUNK):
        rows = slice(r0, r0 + ROW_CHUNK)
        if per_row:
            gs = weight * gt_ref[rows, :] * gout_ref[...]
        o_ref[rows, :] = x_ref[rows, :] + o_ref[rows, :] * _inv_rms(o_ref[rows, :]) * gs


def _mod_kernel(c_ref, w_ref, b_ref, o_ref):
    a = _silu(c_ref[...]).astype(BF16)
    o_ref[...] = _dot(a, w_ref[...].astype(BF16)) + b_ref[...]


def _mod_table(c_all, w_mod, b_mod, tn):
    n_layers, d, n9 = w_mod.shape
    r = c_all.shape[0]
    return pl.pallas_call(
        _mod_kernel,
        out_shape=jax.ShapeDtypeStruct((n_layers, r, n9), F32),
        grid=(n_layers, n9 // tn),
        in_specs=[
            pl.BlockSpec((r, d), lambda l, j: (0, 0)),
            pl.BlockSpec((None, d, tn), lambda l, j: (l, 0, j)),
            pl.BlockSpec((None, 1, tn), lambda l, j: (l, 0, j)),
        ],
        out_specs=pl.BlockSpec((None, r, tn), lambda l, j: (l, 0, j)),
        compiler_params=_params(2),
        name="mod_table",
    )(c_all, w_mod, b_mod.reshape(n_layers, 1, n9))


class _Mod:
    def __init__(self, table, per_row, layer, sub, norm_g4):
        self.table = table
        self.per_row = per_row
        self.layer = layer
        self.sub = sub
        self.norm_g4 = norm_g4

    def spec(self, which, tm, rows_per_group):
        l, idx = self.layer, 3 * self.sub + which
        d = self.table.shape[-1]
        if self.per_row:
            return pl.BlockSpec((None, None, tm, d), lambda i, *_: (l, idx, i, 0))
        return pl.BlockSpec((None, None, None, 1, d),
                            lambda i, *_: (l, (i * tm) // rows_per_group, idx, 0, 0))

    def group_spec(self, which):
        assert not self.per_row
        l, idx = self.layer, 3 * self.sub + which
        d = self.table.shape[-1]
        return pl.BlockSpec((None, None, None, 1, d), lambda b, *_: (l, b, idx, 0, 0))

    def gain_spec(self, which):
        l, idx = self.layer, 2 * self.sub + which
        d = self.norm_g4.shape[-1]
        return pl.BlockSpec((None, None, 1, d), lambda i, *_: (l, idx, 0, 0))


def _ffn_kernel(x_ref, sh_ref, sc_ref, gt_ref, gin_ref, gout_ref, wg_ref, wu_ref, wd_ref,
                o_ref, *rest):
    *rounded, h_ref, a_ref = rest
    j = pl.program_id(1)

    @pl.when(j == 0)
    def _():
        _modulate_rows(x_ref, gin_ref, sc_ref, sh_ref, h_ref)
        o_ref[...] = jnp.zeros(o_ref.shape, F32)

    if rounded:
        for src, dst in zip((wg_ref, wu_ref, wd_ref), rounded):
            dst[...] = src[...].astype(BF16)
        wg_ref, wu_ref, wd_ref = rounded

    g = _dot(h_ref[...], wg_ref[...])
    u = _dot(h_ref[...], wu_ref[...])
    a_ref[...] = (_silu(g) * u).astype(BF16)
    o_ref[...] += _dot(a_ref[...], wd_ref[...])

    @pl.when(j == pl.num_programs(1) - 1)
    def _():
        _gated_residual_rows(x_ref, gt_ref, gout_ref, o_ref, 0.5)


def _ffn(x, mod, weights, l, half, rows_per_group, tm, tf):
    wg, wu, wd = weights
    m, d = x.shape
    f = wg.shape[-1]
    emit = wg.dtype != BF16
    assert not emit or m == tm
    row = pl.BlockSpec((tm, d), lambda i, j: (i, 0))
    out_shape = [jax.ShapeDtypeStruct((m, d), F32)]
    out_specs = [row]
    if emit:
        out_shape += [jax.ShapeDtypeStruct((1, 1, d, f), BF16)] * 2 + [jax.ShapeDtypeStruct((1, 1, f, d), BF16)]
        out_specs += [pl.BlockSpec((None, None, d, tf), lambda i, j: (0, 0, 0, j))] * 2
        out_specs += [pl.BlockSpec((None, None, tf, d), lambda i, j: (0, 0, j, 0))]
    out = pl.pallas_call(
        _ffn_kernel,
        out_shape=out_shape,
        grid=(m // tm, f // tf),
        in_specs=[
            row,
            mod.spec(0, tm, rows_per_group), mod.spec(1, tm, rows_per_group),
            mod.spec(2, tm, rows_per_group),
            mod.gain_spec(0), mod.gain_spec(1),
            pl.BlockSpec((None, None, d, tf), lambda i, j: (l, half, 0, j)),
            pl.BlockSpec((None, None, d, tf), lambda i, j: (l, half, 0, j)),
            pl.BlockSpec((None, None, tf, d), lambda i, j: (l, half, j, 0)),
        ],
        out_specs=out_specs,
        scratch_shapes=[pltpu.VMEM((tm, d), BF16), pltpu.VMEM((tm, tf), BF16)],
        compiler_params=_params(2),
        name="ffn",
    )(x, mod.table, mod.table, mod.table, mod.norm_g4, mod.norm_g4, wg, wu, wd)
    return (out[0], tuple(out[1:])) if emit else (out[0], None)


def _inproj_kernel(x_ref, sh_ref, sc_ref, gin_ref, w_ref, *rest):
    o_refs, h_ref = rest[:-1], rest[-1]
    _modulate_rows(x_ref, gin_ref, sc_ref, sh_ref, h_ref)
    for k, o_ref in enumerate(o_refs):
        tn = o_ref.shape[-1]
        o_ref[...] = _dot(h_ref[...], w_ref[:, k * tn:(k + 1) * tn])


def _inproj(x, mod, w, li, n_out, rows_per_group, tm):
    m, d = x.shape
    n_cols = w.shape[-1]
    tn = n_cols // n_out
    return pl.pallas_call(
        _inproj_kernel,
        out_shape=[jax.ShapeDtypeStruct((m, tn), F32)] * n_out,
        grid=(m // tm,),
        in_specs=[
            pl.BlockSpec((tm, d), lambda i: (i, 0)),
            mod.spec(0, tm, rows_per_group), mod.spec(1, tm, rows_per_group),
            mod.gain_spec(0),
            pl.BlockSpec((None, d, n_cols), lambda i: (li, 0, 0), pipeline_mode=pl.Buffered(1)),
        ],
        out_specs=[pl.BlockSpec((tm, tn), lambda i: (i, 0))] * n_out,
        scratch_shapes=[pltpu.VMEM((tm, d), BF16)],
        compiler_params=_params(1),
        name="inproj",
    )(x, mod.table, mod.table, mod.norm_g4, w)


def _qkv_kernel(x_ref, sh_ref, sc_ref, gin_ref, w_ref, qb_ref, kb_ref, vb_ref, k_ref, v_ref, h_ref):
    _modulate_rows(x_ref, gin_ref, sc_ref, sh_ref, h_ref)
    d = k_ref.shape[-1]
    qb_ref[...] = (_dot(h_ref[...], w_ref[:, 0:d]) * (HEAD_DIM ** -0.5 * LOG2E)).astype(BF16)
    k_ref[...] = _dot(h_ref[...], w_ref[:, d:2 * d])
    kb_ref[...] = k_ref[...].astype(BF16)
    v_ref[...] = _dot(h_ref[...], w_ref[:, 2 * d:3 * d])
    vb_ref[...] = v_ref[...].astype(BF16)


def _qkv_proj(x, mod, w, li, rows_per_group, tm):
    m, d = x.shape
    out = pl.BlockSpec((tm, d), lambda i: (i, 0))
    return pl.pallas_call(
        _qkv_kernel,
        out_shape=[jax.ShapeDtypeStruct((m, d), BF16)] * 3 + [jax.ShapeDtypeStruct((m, d), F32)] * 2,
        grid=(m // tm,),
        in_specs=[
            pl.BlockSpec((tm, d), lambda i: (i, 0)),
            mod.spec(0, tm, rows_per_group), mod.spec(1, tm, rows_per_group),
            mod.gain_spec(0),
            pl.BlockSpec((None, d, 3 * d), lambda i: (li, 0, 0), pipeline_mode=pl.Buffered(1)),
        ],
        out_specs=[out] * 5,
        scratch_shapes=[pltpu.VMEM((tm, d), BF16)],
        compiler_params=_params(1),
        name="qkv_proj",
    )(x, mod.table, mod.table, mod.norm_g4, w)


def _outproj_kernel(y_ref, x_ref, gt_ref, gout_ref, w_ref, o_ref):
    o_ref[...] = _dot(y_ref[...].astype(BF16), w_ref[...])
    _gated_residual_rows(x_ref, gt_ref, gout_ref, o_ref, 1.0)


def _outproj(y, x, mod, w, li, rows_per_group, tm):
    m, d = x.shape
    k = y.shape[-1]
    return pl.pallas_call(
        _outproj_kernel,
        out_shape=jax.ShapeDtypeStruct((m, d), F32),
        grid=(m // tm,),
        in_specs=[
            pl.BlockSpec((tm, k), lambda i: (i, 0)),
            pl.BlockSpec((tm, d), lambda i: (i, 0)),
            mod.spec(2, tm, rows_per_group),
            mod.gain_spec(1),
            pl.BlockSpec((None, k, d), lambda i: (li, 0, 0), pipeline_mode=pl.Buffered(1)),
        ],
        out_specs=pl.BlockSpec((tm, d), lambda i: (i, 0)),
        compiler_params=_params(1),
        name="outproj",
    )(y, x, mod.table, mod.norm_g4, w)


CONV_ROW_CHUNK = 32
CONV_COL_CHUNK = 512


def _conv_kernel(av_ref, ag_ref, gb_ref, gc_ref, z_ref, ha_ref, hb_ref,
                 dwa_ref, ba_ref, lng_ref, lnb_ref, dwb_ref,
                 y_ref, ca_ref, cb_ref, ubuf, zbuf, cbuf, *, ka, kb):
    s = pl.program_id(1)
    ts, c = av_ref.shape
    pa = ubuf.shape[0] - ts
    pb = zbuf.shape[0] - ts

    @pl.when(s == 0)
    def _():
        ubuf[0:pa - (ka - 1), :] = jnp.zeros((pa - (ka - 1), c), F32)
        ubuf[pa - (ka - 1):pa, :] = ha_ref[...]
        zbuf[0:pb - (kb - 1), :] = jnp.zeros((pb - (kb - 1), c), F32)
        zbuf[pb - (kb - 1):pb, :] = hb_ref[...]

    @pl.when(s > 0)
    def _():
        ubuf[0:pa, :] = ubuf[ts:ts + pa, :]
        zbuf[0:pb, :] = zbuf[ts:ts + pb, :]

    ubuf[pa:, :] = av_ref[...] * jax.nn.sigmoid(ag_ref[...])
    zbuf[pb:, :] = gc_ref[...] * z_ref[...]

    rc = min(CONV_ROW_CHUNK, ts)
    cc = min(CONV_COL_CHUNK, c)
    for r0 in range(0, ts, rc):
        for c0 in range(0, c, cc):
            acc = jnp.zeros((rc, cc), F32)
            for k in range(ka):
                start = r0 + k + pa - (ka - 1)
                acc = acc + dwa_ref[k:k + 1, c0:c0 + cc] * ubuf[start:start + rc, c0:c0 + cc]
            cbuf[r0:r0 + rc, c0:c0 + cc] = acc
    ya = cbuf[...] + ba_ref[...]
    mu = jnp.mean(ya, axis=-1, keepdims=True)
    yc = ya - mu
    ya = yc * lax.rsqrt(jnp.mean(yc * yc, axis=-1, keepdims=True) + EPS) * lng_ref[...] + lnb_ref[...]
    y_ref[:, 0:c] = _silu(ya).astype(y_ref.dtype)

    acc = jnp.zeros((ts, c), F32)
    for k in range(kb):
        start = k + pb - (kb - 1)
        acc = acc + dwb_ref[k:k + 1, :] * zbuf[start:start + ts, :]
    y_ref[:, c:2 * c] = (gb_ref[...] * acc).astype(y_ref.dtype)

    @pl.when(s == pl.num_programs(1) - 1)
    def _():
        ca_ref[...] = ubuf[ts + pa - (ka - 1):ts + pa, :]
        cb_ref[...] = zbuf[ts + pb - (kb - 1):ts + pb, :]


def _conv_mix(proj, hist_a, hist_b, dw_a, b_a, ln_g, ln_b, dw_b, li, n, ts):
    c = proj[0].shape[-1]
    s = proj[0].shape[0] // n
    ka, kb = dw_a.shape[1], dw_b.shape[1]
    pa = -(-(ka - 1) // SUBLANES) * SUBLANES
    pb = -(-(kb - 1) // SUBLANES) * SUBLANES
    tile = pl.BlockSpec((None, ts, c), lambda b, t: (b, t, 0))
    vec = pl.BlockSpec((None, 1, c), lambda b, t: (li, 0, 0))
    in_specs = [tile] * 5
    args = [p.reshape(n, s, c) for p in proj]
    in_specs += [pl.BlockSpec((None, None, ka - 1, c), lambda b, t: (li, b, 0, 0)),
                 pl.BlockSpec((None, None, kb - 1, c), lambda b, t: (li, b, 0, 0))]
    args += [hist_a, hist_b]
    in_specs += [pl.BlockSpec((None, ka, c), lambda b, t: (li, 0, 0)), vec, vec, vec,
                 pl.BlockSpec((None, kb, c), lambda b, t: (li, 0, 0))]
    nl = dw_a.shape[0]
    args += [dw_a, b_a.reshape(nl, 1, c), ln_g.reshape(nl, 1, c), ln_b.reshape(nl, 1, c), dw_b]
    y, ca, cb = pl.pallas_call(
        functools.partial(_conv_kernel, ka=ka, kb=kb),
        out_shape=[jax.ShapeDtypeStruct((n, s, 2 * c), F32),
                   jax.ShapeDtypeStruct((n, ka - 1, c), F32),
                   jax.ShapeDtypeStruct((n, kb - 1, c), F32)],
        grid=(n, s // ts),
        in_specs=in_specs,
        out_specs=[pl.BlockSpec((None, ts, 2 * c), lambda b, t: (b, t, 0)),
                   pl.BlockSpec((None, ka - 1, c), lambda b, t: (b, 0, 0)),
                   pl.BlockSpec((None, kb - 1, c), lambda b, t: (b, 0, 0))],
        scratch_shapes=[pltpu.VMEM((ts + pa, c), F32), pltpu.VMEM((ts + pb, c), F32),
                        pltpu.VMEM((ts, c), F32)],
        compiler_params=_params(2),
        name="conv_mix",
    )(*args)
    return y.reshape(n * s, 2 * c), ca, cb


SUBLANES = 8


def _conv_layer_kernel(x_ref, sh_ref, sc_ref, gt_ref, gin_ref, gout_ref, win_ref,
                       dwa_ref, ba_ref, lng_ref, lnb_ref, dwb_ref, wout_ref,
                       o_ref, ca_ref, cb_ref,
                       h_ref, ubuf, zbuf, gb_ref, xs_ref, cbuf, y_ref, *, ka, kb):
    s = pl.program_id(1)
    ts = x_ref.shape[0]
    c = ubuf.shape[1]
    pa = ubuf.shape[0] - ts
    pb = zbuf.shape[0] - ts
    cc = xs_ref.shape[2]
    n_shift_rows = xs_ref.shape[1]

    @pl.when(s == 0)
    def _():
        ubuf[0:pa, :] = jnp.zeros((pa, c), F32)
        zbuf[0:pb, :] = jnp.zeros((pb, c), F32)

    @pl.when(s > 0)
    def _():
        ubuf[0:pa, :] = ubuf[ts:ts + pa, :]
        zbuf[0:pb, :] = zbuf[ts:ts + pb, :]

    _modulate_rows(x_ref, gin_ref, sc_ref, sh_ref, h_ref)

    def proj(part, c0):
        return _dot(h_ref[...], win_ref[:, part * c + c0:part * c + c0 + cc])

    for c0 in range(0, c, cc):
        ubuf[pa:, c0:c0 + cc] = proj(0, c0) * jax.nn.sigmoid(proj(1, c0))

    for c0 in range(0, c, cc):
        cols = slice(c0, c0 + cc)
        for b in range(1, SUBLANES):
            xs_ref[b - 1] = ubuf[b:b + n_shift_rows, cols]
        for r0 in range(0, ts, CONV_ROW_CHUNK):
            acc = jnp.zeros((CONV_ROW_CHUNK, cc), F32)
            for k in range(ka):
                a, b = divmod(k + pa - (ka - 1), SUBLANES)
                start = a * SUBLANES + r0
                if b == 0:
                    src = ubuf[start:start + CONV_ROW_CHUNK, cols]
                else:
                    src = xs_ref[b - 1, start:start + CONV_ROW_CHUNK, :]
                acc = acc + dwa_ref[k:k + 1, cols] * src
            cbuf[r0:r0 + CONV_ROW_CHUNK, cols] = acc + ba_ref[:, cols]
        gb_ref[:, cols] = proj(2, c0)
        zbuf[pb:, cols] = proj(3, c0) * proj(4, c0)

    for r0 in range(0, ts, ROW_CHUNK):
        rows = slice(r0, r0 + ROW_CHUNK)
        ya = cbuf[rows, :]
        yc = ya - jnp.mean(ya, axis=-1, keepdims=True)
        yn = yc * _inv_rms(yc) * lng_ref[...] + lnb_ref[...]
        y_ref[rows, 0:c] = _silu(yn).astype(BF16)
        acc = jnp.zeros((ROW_CHUNK, c), F32)
        for k in range(kb):
            start = r0 + k + pb - (kb - 1)
            acc = acc + dwb_ref[k:k + 1, :] * zbuf[start:start + ROW_CHUNK, :]
        y_ref[rows, c:2 * c] = (gb_ref[rows, :] * acc).astype(BF16)

    o_ref[...] = _dot(y_ref[...], wout_ref[...])
    _gated_residual_rows(x_ref, gt_ref, gout_ref, o_ref, 1.0)

    @pl.when(s == pl.num_programs(1) - 1)
    def _():
        ca_ref[...] = ubuf[ts + pa - (ka - 1):ts + pa, :]
        cb_ref[...] = zbuf[ts + pb - (kb - 1):ts + pb, :]


def _conv_layer(x, mod, w_in, dw_a, b_a, ln_g, ln_b, dw_b, w_out, li, n, ts):
    m, d = x.shape
    s = m // n
    c = dw_a.shape[-1]
    ka, kb = dw_a.shape[1], dw_b.shape[1]
    pa = -(-(ka - 1) // SUBLANES) * SUBLANES
    pb = -(-(kb - 1) // SUBLANES) * SUBLANES
    cc = min(CONV_COL_CHUNK, c)
    nl = dw_a.shape[0]
    tile = pl.BlockSpec((None, ts, d), lambda b, t: (b, t, 0))
    vec = pl.BlockSpec((None, 1, c), lambda b, t: (li, 0, 0))
    resident = functools.partial(pl.BlockSpec, pipeline_mode=pl.Buffered(1))
    o, ca, cb = pl.pallas_call(
        functools.partial(_conv_layer_kernel, ka=ka, kb=kb),
        out_shape=[jax.ShapeDtypeStruct((n, s, d), F32),
                   jax.ShapeDtypeStruct((n, ka - 1, c), F32),
                   jax.ShapeDtypeStruct((n, kb - 1, c), F32)],
        grid=(n, s // ts),
        in_specs=[tile, mod.group_spec(0), mod.group_spec(1), mod.group_spec(2),
                  mod.gain_spec(0), mod.gain_spec(1),
                  resident((None, d, w_in.shape[-1]), lambda b, t: (li, 0, 0)),
                  pl.BlockSpec((None, ka, c), lambda b, t: (li, 0, 0)), vec, vec, vec,
                  pl.BlockSpec((None, kb, c), lambda b, t: (li, 0, 0)),
                  resident((None, 2 * c, d), lambda b, t: (li, 0, 0))],
        out_specs=[tile,
                   pl.BlockSpec((None, ka - 1, c), lambda b, t: (b, 0, 0)),
                   pl.BlockSpec((None, kb - 1, c), lambda b, t: (b, 0, 0))],
        scratch_shapes=[pltpu.VMEM((ts, d), BF16),
                        pltpu.VMEM((ts + pa, c), F32), pltpu.VMEM((ts + pb, c), F32),
                        pltpu.VMEM((ts, c), F32),
                        pltpu.VMEM((SUBLANES - 1, ts + pa - SUBLANES, cc), F32),
                        pltpu.VMEM((ts, c), F32), pltpu.VMEM((ts, 2 * c), BF16)],
        compiler_params=_params(2),
        name="conv_layer",
    )(x.reshape(n, s, d), mod.table, mod.table, mod.table, mod.norm_g4, mod.norm_g4, w_in,
      dw_a, b_a.reshape(nl, 1, c), ln_g.reshape(nl, 1, c), ln_b.reshape(nl, 1, c), dw_b, w_out)
    return o.reshape(m, d), ca, cb


def _multiplicity(dist, xp):
    cnt = 0
    for window, dil in DIL_PATTERNS:
        assert dil & (dil - 1) == 0
        hit = (dist >= 0) & (dist <= window) & ((dist & (dil - 1)) == 0)
        cnt = cnt + hit.astype(xp.int32)
    return cnt


def _log_multiplicity(dist):
    cnt = _multiplicity(dist, jnp)
    return jnp.where(cnt > 0, jnp.log(jnp.maximum(cnt, 1).astype(F32)), NEG)


def _attn_prompt_kernel(q_ref, k_ref, v_ref, slope_ref, o_ref,
                        vb_ref, s_ref, p_ref, bias_ref, logc_ref, dist_ref, *, tq):
    s_len, dh = q_ref.shape
    nq = s_len // tq

    @pl.when((pl.program_id(0) == 0) & (pl.program_id(1) == 0))
    def _():
        dist = (lax.broadcasted_iota(jnp.int32, (tq, s_len), 0) + (s_len - tq)
                - lax.broadcasted_iota(jnp.int32, (tq, s_len), 1))
        logc_ref[...] = _log_multiplicity(dist) * LOG2E
        dist_ref[...] = dist.astype(F32) * LOG2E
        vb_ref[:, dh:] = jnp.ones((s_len, dh), BF16)

    @pl.when(pl.program_id(1) == 0)
    def _():
        bias_ref[...] = logc_ref[...] - slope_ref[...] * dist_ref[...]

    vb_ref[:, :dh] = v_ref[...]

    def scores(i):
        c0 = (nq - 1 - i) * tq
        q = q_ref[i * tq:(i + 1) * tq, :]
        m = jnp.full((tq, 1), NEG, F32)
        for j in range(i + 1):
            keys = slice(j * tq, (j + 1) * tq)
            s = _dot_nt(q, k_ref[keys, :]) + bias_ref[:, c0 + j * tq:c0 + (j + 1) * tq]
            s_ref[i % 2, :, keys] = s
            m = jnp.maximum(m, jnp.max(s, axis=-1, keepdims=True))
        return m

    m_next = scores(0)
    for i in range(nq):
        slot, m = i % 2, m_next
        if i + 1 < nq:
            m_next = scores(i + 1)
        for j in range(i + 1):
            keys = slice(j * tq, (j + 1) * tq)
            p_ref[slot, :, keys] = jnp.exp2(s_ref[slot, :, keys] - m).astype(BF16)
        n_keys = (i + 1) * tq
        ol = _dot(p_ref[slot, :, :n_keys], vb_ref[:n_keys, :])
        o_ref[i * tq:(i + 1) * tq, :] = (ol[:, :dh] / ol[:, dh:]).astype(o_ref.dtype)


def _alibi_slopes(n_heads):
    return jnp.exp2(-8.0 * jnp.arange(1, n_heads + 1, dtype=F32) / n_heads)


def _attn_prompt(q, k, v, n, tq):
    m, d = q.shape
    s = m // n
    n_heads = d // HEAD_DIM
    slopes = jnp.broadcast_to(_alibi_slopes(n_heads)[:, None, None], (n_heads, 1, s))
    head = pl.BlockSpec((s, HEAD_DIM), lambda h, b: (b, h))
    return pl.pallas_call(
        functools.partial(_attn_prompt_kernel, tq=tq),
        out_shape=jax.ShapeDtypeStruct((m, d), BF16),
        grid=(n_heads, n),
        in_specs=[head, head, head, pl.BlockSpec((None, 1, s), lambda h, b: (h, 0, 0))],
        out_specs=head,
        scratch_shapes=[pltpu.VMEM((s, 2 * HEAD_DIM), BF16),
                        pltpu.VMEM((2, tq, s), F32), pltpu.VMEM((2, tq, s), BF16),
                        pltpu.VMEM((tq, s), F32), pltpu.VMEM((tq, s), F32), pltpu.VMEM((tq, s), F32)],
        compiler_params=_params(2),
        name="attn_prompt",
    )(q, k, v, slopes)


def _sample_regions(wb, t_new):
    dil_max = max(dil for _, dil in DIL_PATTERNS)
    reach = max(window for window, dil in DIL_PATTERNS if dil < dil_max)
    near = -(-reach // dil_max) * dil_max
    assert wb % dil_max == 0 and wb > near and t_new <= dil_max
    r = np.arange(wb - near)
    skipped = r[(r % dil_max) >= t_new]
    for t in range(t_new):
        assert not _multiplicity(wb + t - skipped, np).any()
    return dil_max, near


def _shift(x, n):
    assert n & (n - 1) == 0
    return x >> (n.bit_length() - 1), x & (n - 1)


def _attn_sample_kernel(q_ref, kfar_ref, vfar_ref, knear_ref, vnear_ref, knew_ref, vnew_ref,
                        o_ref, bfar_ref, bnear_ref, bnew_ref, *, t_new, n_heads, wb, dil_max):
    n_far = kfar_ref.shape[0] * kfar_ref.shape[1] * kfar_ref.shape[2]
    n_near = knear_ref.shape[0]
    dh = q_ref.shape[-1]

    @pl.when(pl.program_id(0) == 0)
    def _():
        def table(ref, row_of_slot):
            i = lax.broadcasted_iota(jnp.int32, ref.shape, 0)
            c = lax.broadcasted_iota(jnp.int32, ref.shape, 1)
            h, t = _shift(i, t_new)
            slot, hk = _shift(c, n_heads)
            dist = wb + t - row_of_slot(slot)
            slope = jnp.exp2(-8.0 * (h + 1).astype(F32) / n_heads)
            ref[...] = jnp.where(h == hk, _log_multiplicity(dist) - slope * dist.astype(F32), NEG)

        def far_row(slot):
            g, p = _shift(slot, t_new)
            return g * dil_max + p

        table(bfar_ref, far_row)
        table(bnear_ref, lambda slot: slot + (wb - n_near // n_heads))
        table(bnew_ref, lambda slot: slot + wb)

    q = (q_ref[...] * (dh ** -0.5)).astype(BF16)
    kfar = kfar_ref[...].reshape(n_far, dh).astype(BF16)
    s_far = _dot_nt(q, kfar) + bfar_ref[...]
    s_near = _dot_nt(q, knear_ref[...].astype(BF16)) + bnear_ref[...]
    s_new = _dot_nt(q, knew_ref[...].astype(BF16)) + bnew_ref[...]
    m = jnp.maximum(jnp.maximum(jnp.max(s_far, axis=-1, keepdims=True),
                                jnp.max(s_near, axis=-1, keepdims=True)),
                    jnp.max(s_new, axis=-1, keepdims=True))
    p_far = jnp.exp(s_far - m)
    p_near = jnp.exp(s_near - m)
    p_new = jnp.exp(s_new - m)
    l = (jnp.sum(p_far, axis=-1, keepdims=True) + jnp.sum(p_near, axis=-1, keepdims=True)
         + jnp.sum(p_new, axis=-1, keepdims=True))
    vfar = vfar_ref[...].reshape(n_far, dh).astype(BF16)
    o = (_dot(p_far.astype(BF16), vfar) + _dot(p_near.astype(BF16), vnear_ref[...].astype(BF16))
         + _dot(p_new.astype(BF16), vnew_ref[...].astype(BF16)))
    o_ref[...] = o / l


def _attn_sample(q, k_new, v_new, cache_k, cache_v, li, nb):
    m, d = q.shape
    t = m // nb
    n_layers, _, wb, n_heads, dh = cache_k.shape
    dil_max, near = _sample_regions(wb, t)
    n_groups = (wb - near) // dil_max
    by_head = lambda a: a.reshape(nb, t, n_heads, dh).transpose(0, 2, 1, 3).reshape(nb, n_heads * t, dh)
    rows = lambda a: a.reshape(nb, t * n_heads, dh)
    far = lambda c: c.reshape(n_layers, nb, wb // dil_max, dil_max, n_heads, dh)
    flat = lambda c: c.reshape(n_layers, nb, wb * n_heads, dh)
    qspec = pl.BlockSpec((None, n_heads * t, dh), lambda b: (b, 0, 0))
    far_spec = pl.BlockSpec((None, None, n_groups, t, n_heads, dh), lambda b: (li, b, 0, 0, 0, 0))
    near_spec = pl.BlockSpec((None, None, near * n_heads, dh), lambda b: (li, b, wb // near - 1, 0))
    assert wb % near == 0
    o = pl.pallas_call(
        functools.partial(_attn_sample_kernel, t_new=t, n_heads=n_heads, wb=wb, dil_max=dil_max),
        out_shape=jax.ShapeDtypeStruct((nb, n_heads * t, dh), F32),
        grid=(nb,),
        in_specs=[qspec, far_spec, far_spec, near_spec, near_spec, qspec, qspec],
        out_specs=qspec,
        scratch_shapes=[pltpu.VMEM((n_heads * t, n_groups * t * n_heads), F32),
                        pltpu.VMEM((n_heads * t, near * n_heads), F32),
                        pltpu.VMEM((n_heads * t, t * n_heads), F32)],
        compiler_params=_params(1),
        name="attn_sample",
    )(by_head(q), far(cache_k), far(cache_v), flat(cache_k), flat(cache_v), rows(k_new), rows(v_new))
    return o.reshape(nb, n_heads, t, dh).transpose(0, 2, 1, 3).reshape(m, d)


def _trunk(x3, mod_table, per_row, w, ffn_weights, caches, tiles):
    tm, tf, tp = tiles["ffn_rows"], tiles["ffn_hidden"], tiles["proj_rows"]
    n, s, d = x3.shape
    x = x3.reshape(n * s, d)
    depth = w["norm_g4"].shape[0]
    n_heads = d // HEAD_DIM
    new_a, new_b, new_k, new_v = [], [], [], []
    rounded = {}
    for layer in range(depth):
        li = layer // 2
        mods = [_Mod(mod_table, per_row, layer, sub, w["norm_g4"]) for sub in range(3)]
        x, rounded[layer, 0] = _ffn(x, mods[0], *ffn_weights[layer, 0], s, tm, tf)
        if layer % 2 == 0 and not per_row:
            x, ca, cb = _conv_layer(x, mods[1], w["conv_w_in"], w["conv_dw_a"], w["conv_b_a"],
                                    w["conv_ln_g"], w["conv_ln_b"], w["conv_dw_b"],
                                    w["conv_w_out"], li, n, tiles["conv_rows"])
            new_a.append(ca)
            new_b.append(cb)
        elif layer % 2 == 0:
            proj = _inproj(x, mods[1], w["conv_w_in"], li, 5, s, tp)
            y, ca, cb = _conv_mix(proj, caches[0], caches[1], w["conv_dw_a"], w["conv_b_a"],
                                  w["conv_ln_g"], w["conv_ln_b"], w["conv_dw_b"], li, n,
                                  tiles["conv_rows"])
            new_a.append(ca)
            new_b.append(cb)
            x = _outproj(y, x, mods[1], w["conv_w_out"], li, s, tp)
        else:
            if per_row:
                q, k, v = _inproj(x, mods[1], w["attn_w_qkv"], li, 3, s, tp)
                o = _attn_sample(q, k, v, caches[2], caches[3], li, n)
                keep = s
            else:
                qb, kb, vb, k, v = _qkv_proj(x, mods[1], w["attn_w_qkv"], li, s, tp)
                o = _attn_prompt(qb, kb, vb, n, tiles["attn_rows"])
                keep = min(DIL_PATTERNS[-1][0], s)
            new_k.append(k.reshape(n, s, n_heads, HEAD_DIM)[:, s - keep:])
            new_v.append(v.reshape(n, s, n_heads, HEAD_DIM)[:, s - keep:])
            x = _outproj(o, x, mods[1], w["attn_w_o"], li, s, tp)
        x, rounded[layer, 1] = _ffn(x, mods[2], *ffn_weights[layer, 1], s, tm, tf)
    outs = (x.reshape(n, s, d), jnp.stack(new_a), jnp.stack(new_b), jnp.stack(new_k), jnp.stack(new_v))
    return outs, rounded


def _tile(total, want):
    t = min(total, want)
    while total % t:
        t //= 2
    return t


def kernel(x_prompt, x_sample, cache_conv_a, cache_conv_b, cache_k, cache_v, c_prompt, c_sample,
           w_mod, b_mod, norm_g, ffn_w_gate, ffn_w_up, ffn_w_down, conv_w_in, conv_dw_a, conv_b_a,
           conv_ln_g, conv_ln_b, conv_dw_b, conv_w_out, attn_w_qkv, attn_w_o):
    n, s, d = x_prompt.shape
    nb, t, _ = x_sample.shape
    depth = w_mod.shape[0]

    mod = _mod_table(jnp.concatenate([c_prompt, c_sample], axis=0), w_mod, b_mod,
                     tn=_tile(w_mod.shape[-1], 1024))
    mod_p = mod[:, :n].reshape(depth, n, N_MOD, 1, d)
    mod_s = jnp.repeat(mod[:, n:].reshape(depth, nb, N_MOD, d), t, axis=1).transpose(0, 2, 1, 3)

    w = dict(
        norm_g4=norm_g.reshape(depth, 6, 1, d),
        conv_w_in=conv_w_in.astype(BF16), conv_w_out=conv_w_out.astype(BF16),
        attn_w_qkv=attn_w_qkv.astype(BF16), attn_w_o=attn_w_o.astype(BF16),
        conv_dw_a=conv_dw_a, conv_b_a=conv_b_a, conv_ln_g=conv_ln_g, conv_ln_b=conv_ln_b,
        conv_dw_b=conv_dw_b,
    )
    tf = _tile(ffn_w_gate.shape[-1], 512)
    tiles_prompt = dict(ffn_rows=_tile(s, 1024), ffn_hidden=tf, proj_rows=_tile(s, 256),
                        conv_rows=_tile(s, 256), attn_rows=_tile(s, 256))
    tiles_sample = dict(ffn_rows=nb * t, ffn_hidden=tf, proj_rows=nb * t, conv_rows=t)
    ffn_f32 = {(l, h): ((ffn_w_gate, ffn_w_up, ffn_w_down), l, h) for l in range(depth) for h in range(2)}
    (ys, as_, bs, ks, vs), rounded = _trunk(x_sample, mod_s, True, w, ffn_f32,
                                            (cache_conv_a, cache_conv_b, cache_k, cache_v), tiles_sample)
    ffn_bf16 = {key: (copies, 0, 0) for key, copies in rounded.items()}
    (yp, ap, bp, kp, vp), _ = _trunk(x_prompt, mod_p, False, w, ffn_bf16, None, tiles_prompt)
    return (yp, ys, ap, as_, bp, bs, kp, vp, ks, vs)
```

```python
import functools

import numpy as np
import jax
import jax.numpy as jnp
from jax import lax
from jax.experimental import pallas as pl
from jax.experimental.pallas import tpu as pltpu

F32 = jnp.float32
BF16 = jnp.bfloat16

HEAD_DIM = 128
DIL_PATTERNS = ((128, 1), (512, 4), (2048, 16))
N_MOD = 9
EPS = 1e-6
NEG = -1e30
LOG2E = 1.4426950408889634

V7X_VMEM_BYTES = 64 * 1024 * 1024
VMEM_LIMIT_BYTES = V7X_VMEM_BYTES - 8 * 1024 * 1024
ROW_CHUNK = 16


def _params(n_axes):
    return pltpu.CompilerParams(
        dimension_semantics=("arbitrary",) * n_axes,
        vmem_limit_bytes=VMEM_LIMIT_BYTES,
    )


def _silu(x):
    return x * jax.nn.sigmoid(x)


def _dot(a, b):
    return jnp.dot(a, b, preferred_element_type=F32)


def _dot_nt(a, b):
    return lax.dot_general(a, b, (((1,), (1,)), ((), ())), preferred_element_type=F32)


def _inv_rms(x):
    return lax.rsqrt(jnp.mean(jnp.square(x), axis=-1, keepdims=True) + EPS)


def _mod_rows(ref, rows):
    return ref[rows, :] if ref.shape[0] > 1 else ref[...]


def _modulate_rows(x_ref, gin_ref, sc_ref, sh_ref, h_ref):
    per_row = sc_ref.shape[0] > 1
    gs = None if per_row else gin_ref[...] * (1.0 + sc_ref[...])
    for r0 in range(0, x_ref.shape[0], ROW_CHUNK):
        rows = slice(r0, r0 + ROW_CHUNK)
        if per_row:
            gs = gin_ref[...] * (1.0 + sc_ref[rows, :])
        h = x_ref[rows, :] * _inv_rms(x_ref[rows, :]) * gs + _mod_rows(sh_ref, rows)
        h_ref[rows, :] = h.astype(h_ref.dtype)


def _gated_residual_rows(x_ref, gt_ref, gout_ref, o_ref, weight):
    per_row = gt_ref.shape[0] > 1
    gs = None if per_row else weight * gt_ref[...] * gout_ref[...]
    for r0 in range(0, x_ref.shape[0], ROW_CHUNK):
        rows = slice(r0, r0 + ROW_CHUNK)
        if per_row:
            gs = weight * gt_ref[rows, :] * gout_ref[...]
        o_ref[rows, :] = x_ref[rows, :] + o_ref[rows, :] * _inv_rms(o_ref[rows, :]) * gs


def _mod_kernel(c_ref, w_ref, b_ref, o_ref):
    a = _silu(c_ref[...]).astype(BF16)
    o_ref[...] = _dot(a, w_ref[...].astype(BF16)) + b_ref[...]


def _mod_table(c_all, w_mod, b_mod, tn):
    n_layers, d, n9 = w_mod.shape
    r = c_all.shape[0]
    return pl.pallas_call(
        _mod_kernel,
        out_shape=jax.ShapeDtypeStruct((n_layers, r, n9), F32),
        grid=(n_layers, n9 // tn),
        in_specs=[
            pl.BlockSpec((r, d), lambda l, j: (0, 0)),
            pl.BlockSpec((None, d, tn), lambda l, j: (l, 0, j)),
            pl.BlockSpec((None, 1, tn), lambda l, j: (l, 0, j)),
        ],
        out_specs=pl.BlockSpec((None, r, tn), lambda l, j: (l, 0, j)),
        compiler_params=_params(2),
        name="mod_table",
    )(c_all, w_mod, b_mod.reshape(n_layers, 1, n9))


class _Mod:
    def __init__(self, table, per_row, layer, sub, norm_g4):
        self.table = table
        self.per_row = per_row
        self.layer = layer
        self.sub = sub
        self.norm_g4 = norm_g4

    def spec(self, which, tm, rows_per_group):
        l, idx = self.layer, 3 * self.sub + which
        d = self.table.shape[-1]
        if self.per_row:
            return pl.BlockSpec((None, None, tm, d), lambda i, *_: (l, idx, i, 0))
        return pl.BlockSpec((None, None, None, 1, d),
                            lambda i, *_: (l, (i * tm) // rows_per_group, idx, 0, 0))

    def whole_spec(self, which):
        assert self.per_row
        l, idx = self.layer, 3 * self.sub + which
        _, _, m, d = self.table.shape
        return pl.BlockSpec((None, None, m, d), lambda *_: (l, idx, 0, 0))

    def group_spec(self, which):
        assert not self.per_row
        l, idx = self.layer, 3 * self.sub + which
        d = self.table.shape[-1]
        return pl.BlockSpec((None, None, None, 1, d), lambda b, *_: (l, b, idx, 0, 0))

    def gain_spec(self, which):
        l, idx = self.layer, 2 * self.sub + which
        d = self.norm_g4.shape[-1]
        return pl.BlockSpec((None, None, 1, d), lambda i, *_: (l, idx, 0, 0))


def _ffn_kernel(x_ref, sh_ref, sc_ref, gt_ref, gin_ref, gout_ref, wg_ref, wu_ref, wd_ref,
                o_ref, *rest):
    *rounded, h_ref, a_ref = rest
    j = pl.program_id(1)
    last = pl.num_programs(1) - 1

    def hidden_tile(first):
        wg, wu, wd = wg_ref, wu_ref, wd_ref
        if rounded:
            for src, dst in zip((wg_ref, wu_ref, wd_ref), rounded):
                dst[...] = src[...].astype(BF16)
            wg, wu, wd = rounded
        g = _dot(h_ref[...], wg[...])
        u = _dot(h_ref[...], wu[...])
        a_ref[...] = (_silu(g) * u).astype(BF16)
        if first:
            o_ref[...] = _dot(a_ref[...], wd[...])
        else:
            o_ref[...] += _dot(a_ref[...], wd[...])

    @pl.when(j == 0)
    def _():
        _modulate_rows(x_ref, gin_ref, sc_ref, sh_ref, h_ref)
        hidden_tile(True)

    @pl.when((j > 0) & (j < last))
    def _():
        hidden_tile(False)

    @pl.when(j == last)
    def _():
        hidden_tile(False)
        _gated_residual_rows(x_ref, gt_ref, gout_ref, o_ref, 0.5)


def _ffn(x, mod, weights, l, half, rows_per_group, tm, tf):
    wg, wu, wd = weights
    m, d = x.shape
    f = wg.shape[-1]
    emit = wg.dtype != BF16
    assert not emit or m == tm
    assert f // tf >= 2
    row = pl.BlockSpec((tm, d), lambda i, j: (i, 0))
    out_shape = [jax.ShapeDtypeStruct((m, d), F32)]
    out_specs = [row]
    if emit:
        out_shape += [jax.ShapeDtypeStruct((1, 1, d, f), BF16)] * 2 + [jax.ShapeDtypeStruct((1, 1, f, d), BF16)]
        out_specs += [pl.BlockSpec((None, None, d, tf), lambda i, j: (0, 0, 0, j))] * 2
        out_specs += [pl.BlockSpec((None, None, tf, d), lambda i, j: (0, 0, j, 0))]
    out = pl.pallas_call(
        _ffn_kernel,
        out_shape=out_shape,
        grid=(m // tm, f // tf),
        in_specs=[
            row,
            mod.spec(0, tm, rows_per_group), mod.spec(1, tm, rows_per_group),
            mod.spec(2, tm, rows_per_group),
            mod.gain_spec(0), mod.gain_spec(1),
            pl.BlockSpec((None, None, d, tf), lambda i, j: (l, half, 0, j)),
            pl.BlockSpec((None, None, d, tf), lambda i, j: (l, half, 0, j)),
            pl.BlockSpec((None, None, tf, d), lambda i, j: (l, half, j, 0)),
        ],
        out_specs=out_specs,
        scratch_shapes=[pltpu.VMEM((tm, d), BF16), pltpu.VMEM((tm, tf), BF16)],
        compiler_params=_params(2),
        name="ffn",
    )(x, mod.table, mod.table, mod.table, mod.norm_g4, mod.norm_g4, wg, wu, wd)
    return (out[0], tuple(out[1:])) if emit else (out[0], None)


def _inproj_stream_kernel(x_ref, sh_ref, sc_ref, gin_ref, w_ref, o_ref, wb_ref, h_ref):
    @pl.when(pl.program_id(0) == 0)
    def _():
        _modulate_rows(x_ref, gin_ref, sc_ref, sh_ref, h_ref)

    wb_ref[...] = w_ref[...].astype(BF16)
    o_ref[...] = _dot(h_ref[...], wb_ref[...])


def _inproj_stream(x, mod, w, li, n_out, cols_per_step):
    m, d = x.shape
    n_cols = w.shape[-1]
    tn = _tile(n_cols, cols_per_step)
    whole = pl.BlockSpec((m, d), lambda j: (0, 0))
    o, wb = pl.pallas_call(
        _inproj_stream_kernel,
        out_shape=[jax.ShapeDtypeStruct((m, n_cols), F32), jax.ShapeDtypeStruct((1, d, n_cols), BF16)],
        grid=(n_cols // tn,),
        in_specs=[whole, mod.whole_spec(0), mod.whole_spec(1), mod.gain_spec(0),
                  pl.BlockSpec((None, d, tn), lambda j: (li, 0, j))],
        out_specs=[pl.BlockSpec((m, tn), lambda j: (0, j)),
                   pl.BlockSpec((None, d, tn), lambda j: (0, 0, j))],
        scratch_shapes=[pltpu.VMEM((m, d), BF16)],
        compiler_params=_params(1),
        name="inproj_stream",
    )(x, mod.table, mod.table, mod.norm_g4, w)
    return jnp.split(o, n_out, axis=1), wb


def _outproj_stream_kernel(y_ref, x_ref, gt_ref, gout_ref, w_ref, o_ref, wb_ref):
    j = pl.program_id(0)

    @pl.when(j == 0)
    def _():
        o_ref[...] = jnp.zeros(o_ref.shape, F32)

    wb_ref[...] = w_ref[...].astype(BF16)
    o_ref[...] += _dot(y_ref[...].astype(BF16), wb_ref[...])

    @pl.when(j == pl.num_programs(0) - 1)
    def _():
        _gated_residual_rows(x_ref, gt_ref, gout_ref, o_ref, 1.0)


def _outproj_stream(y, x, mod, w, li, depth_per_step):
    m, d = x.shape
    k = y.shape[-1]
    tk = _tile(k, depth_per_step)
    whole = pl.BlockSpec((m, d), lambda j: (0, 0))
    return pl.pallas_call(
        _outproj_stream_kernel,
        out_shape=[jax.ShapeDtypeStruct((m, d), F32), jax.ShapeDtypeStruct((1, k, d), BF16)],
        grid=(k // tk,),
        in_specs=[pl.BlockSpec((m, tk), lambda j: (0, j)), whole, mod.whole_spec(2), mod.gain_spec(1),
                  pl.BlockSpec((None, tk, d), lambda j: (li, j, 0))],
        out_specs=[whole, pl.BlockSpec((None, tk, d), lambda j: (0, j, 0))],
        compiler_params=_params(1),
        name="outproj_stream",
    )(y, x, mod.table, mod.norm_g4, w)


def _qkv_kernel(x_ref, sh_ref, sc_ref, gin_ref, w_ref, qb_ref, kb_ref, vb_ref, k_ref, v_ref, h_ref):
    _modulate_rows(x_ref, gin_ref, sc_ref, sh_ref, h_ref)
    d = k_ref.shape[-1]
    qb_ref[...] = (_dot(h_ref[...], w_ref[:, 0:d]) * (HEAD_DIM ** -0.5 * LOG2E)).astype(BF16)
    k_ref[...] = _dot(h_ref[...], w_ref[:, d:2 * d])
    kb_ref[...] = k_ref[...].astype(BF16)
    v_ref[...] = _dot(h_ref[...], w_ref[:, 2 * d:3 * d])
    vb_ref[...] = v_ref[...].astype(BF16)


def _qkv_proj(x, mod, w, li, rows_per_group, tm):
    m, d = x.shape
    out = pl.BlockSpec((tm, d), lambda i: (i, 0))
    return pl.pallas_call(
        _qkv_kernel,
        out_shape=[jax.ShapeDtypeStruct((m, d), BF16)] * 3 + [jax.ShapeDtypeStruct((m, d), F32)] * 2,
        grid=(m // tm,),
        in_specs=[
            pl.BlockSpec((tm, d), lambda i: (i, 0)),
            mod.spec(0, tm, rows_per_group), mod.spec(1, tm, rows_per_group),
            mod.gain_spec(0),
            pl.BlockSpec((None, d, 3 * d), lambda i: (li, 0, 0), pipeline_mode=pl.Buffered(1)),
        ],
        out_specs=[out] * 5,
        scratch_shapes=[pltpu.VMEM((tm, d), BF16)],
        compiler_params=_params(1),
        name="qkv_proj",
    )(x, mod.table, mod.table, mod.norm_g4, w)


def _outproj_kernel(y_ref, x_ref, gt_ref, gout_ref, w_ref, o_ref):
    o_ref[...] = _dot(y_ref[...].astype(BF16), w_ref[...])
    _gated_residual_rows(x_ref, gt_ref, gout_ref, o_ref, 1.0)


def _outproj(y, x, mod, w, li, rows_per_group, tm):
    m, d = x.shape
    k = y.shape[-1]
    return pl.pallas_call(
        _outproj_kernel,
        out_shape=jax.ShapeDtypeStruct((m, d), F32),
        grid=(m // tm,),
        in_specs=[
            pl.BlockSpec((tm, k), lambda i: (i, 0)),
            pl.BlockSpec((tm, d), lambda i: (i, 0)),
            mod.spec(2, tm, rows_per_group),
            mod.gain_spec(1),
            pl.BlockSpec((None, k, d), lambda i: (li, 0, 0), pipeline_mode=pl.Buffered(1)),
        ],
        out_specs=pl.BlockSpec((tm, d), lambda i: (i, 0)),
        compiler_params=_params(1),
        name="outproj",
    )(y, x, mod.table, mod.norm_g4, w)


CONV_ROW_CHUNK = 32
CONV_COL_CHUNK = 512


def _conv_kernel(av_ref, ag_ref, gb_ref, gc_ref, z_ref, ha_ref, hb_ref,
                 dwa_ref, ba_ref, lng_ref, lnb_ref, dwb_ref,
                 y_ref, ca_ref, cb_ref, ubuf, zbuf, cbuf, *, ka, kb):
    s = pl.program_id(1)
    ts, c = av_ref.shape
    pa = ubuf.shape[0] - ts
    pb = zbuf.shape[0] - ts

    @pl.when(s == 0)
    def _():
        ubuf[0:pa - (ka - 1), :] = jnp.zeros((pa - (ka - 1), c), F32)
        ubuf[pa - (ka - 1):pa, :] = ha_ref[...]
        zbuf[0:pb - (kb - 1), :] = jnp.zeros((pb - (kb - 1), c), F32)
        zbuf[pb - (kb - 1):pb, :] = hb_ref[...]

    @pl.when(s > 0)
    def _():
        ubuf[0:pa, :] = ubuf[ts:ts + pa, :]
        zbuf[0:pb, :] = zbuf[ts:ts + pb, :]

    ubuf[pa:, :] = av_ref[...] * jax.nn.sigmoid(ag_ref[...])
    zbuf[pb:, :] = gc_ref[...] * z_ref[...]

    rc = min(CONV_ROW_CHUNK, ts)
    cc = min(CONV_COL_CHUNK, c)
    for r0 in range(0, ts, rc):
        for c0 in range(0, c, cc):
            acc = jnp.zeros((rc, cc), F32)
            for k in range(ka):
                start = r0 + k + pa - (ka - 1)
                acc = acc + dwa_ref[k:k + 1, c0:c0 + cc] * ubuf[start:start + rc, c0:c0 + cc]
            cbuf[r0:r0 + rc, c0:c0 + cc] = acc
    ya = cbuf[...] + ba_ref[...]
    mu = jnp.mean(ya, axis=-1, keepdims=True)
    yc = ya - mu
    ya = yc * lax.rsqrt(jnp.mean(yc * yc, axis=-1, keepdims=True) + EPS) * lng_ref[...] + lnb_ref[...]
    y_ref[:, 0:c] = _silu(ya).astype(y_ref.dtype)

    acc = jnp.zeros((ts, c), F32)
    for k in range(kb):
        start = k + pb - (kb - 1)
        acc = acc + dwb_ref[k:k + 1, :] * zbuf[start:start + ts, :]
    y_ref[:, c:2 * c] = (gb_ref[...] * acc).astype(y_ref.dtype)

    @pl.when(s == pl.num_programs(1) - 1)
    def _():
        ca_ref[...] = ubuf[ts + pa - (ka - 1):ts + pa, :]
        cb_ref[...] = zbuf[ts + pb - (kb - 1):ts + pb, :]


def _conv_mix(proj, hist_a, hist_b, dw_a, b_a, ln_g, ln_b, dw_b, li, n, ts):
    c = proj[0].shape[-1]
    s = proj[0].shape[0] // n
    ka, kb = dw_a.shape[1], dw_b.shape[1]
    pa = -(-(ka - 1) // SUBLANES) * SUBLANES
    pb = -(-(kb - 1) // SUBLANES) * SUBLANES
    tile = pl.BlockSpec((None, ts, c), lambda b, t: (b, t, 0))
    vec = pl.BlockSpec((None, 1, c), lambda b, t: (li, 0, 0))
    in_specs = [tile] * 5
    args = [p.reshape(n, s, c) for p in proj]
    in_specs += [pl.BlockSpec((None, None, ka - 1, c), lambda b, t: (li, b, 0, 0)),
                 pl.BlockSpec((None, None, kb - 1, c), lambda b, t: (li, b, 0, 0))]
    args += [hist_a, hist_b]
    in_specs += [pl.BlockSpec((None, ka, c), lambda b, t: (li, 0, 0)), vec, vec, vec,
                 pl.BlockSpec((None, kb, c), lambda b, t: (li, 0, 0))]
    nl = dw_a.shape[0]
    args += [dw_a, b_a.reshape(nl, 1, c), ln_g.reshape(nl, 1, c), ln_b.reshape(nl, 1, c), dw_b]
    y, ca, cb = pl.pallas_call(
        functools.partial(_conv_kernel, ka=ka, kb=kb),
        out_shape=[jax.ShapeDtypeStruct((n, s, 2 * c), F32),
                   jax.ShapeDtypeStruct((n, ka - 1, c), F32),
                   jax.ShapeDtypeStruct((n, kb - 1, c), F32)],
        grid=(n, s // ts),
        in_specs=in_specs,
        out_specs=[pl.BlockSpec((None, ts, 2 * c), lambda b, t: (b, t, 0)),
                   pl.BlockSpec((None, ka - 1, c), lambda b, t: (b, 0, 0)),
                   pl.BlockSpec((None, kb - 1, c), lambda b, t: (b, 0, 0))],
        scratch_shapes=[pltpu.VMEM((ts + pa, c), F32), pltpu.VMEM((ts + pb, c), F32),
                        pltpu.VMEM((ts, c), F32)],
        compiler_params=_params(2),
        name="conv_mix",
    )(*args)
    return y.reshape(n * s, 2 * c), ca, cb


SUBLANES = 8


def _conv_layer_kernel(x_ref, sh_ref, sc_ref, gt_ref, gin_ref, gout_ref, win_ref,
                       dwa_ref, ba_ref, lng_ref, lnb_ref, dwb_ref, wout_ref,
                       o_ref, ca_ref, cb_ref,
                       h_ref, ubuf, zbuf, gb_ref, xs_ref, cbuf, y_ref, *, ka, kb):
    s = pl.program_id(1)
    ts = x_ref.shape[0]
    c = ubuf.shape[1]
    pa = ubuf.shape[0] - ts
    pb = zbuf.shape[0] - ts
    cc = xs_ref.shape[2]
    n_shift_rows = xs_ref.shape[1]

    @pl.when(s == 0)
    def _():
        ubuf[0:pa, :] = jnp.zeros((pa, c), F32)
        zbuf[0:pb, :] = jnp.zeros((pb, c), F32)

    @pl.when(s > 0)
    def _():
        ubuf[0:pa, :] = ubuf[ts:ts + pa, :]
        zbuf[0:pb, :] = zbuf[ts:ts + pb, :]

    _modulate_rows(x_ref, gin_ref, sc_ref, sh_ref, h_ref)

    def proj(part, c0):
        return _dot(h_ref[...], win_ref[:, part * c + c0:part * c + c0 + cc])

    for c0 in range(0, c, cc):
        ubuf[pa:, c0:c0 + cc] = proj(0, c0) * jax.nn.sigmoid(proj(1, c0))

    for c0 in range(0, c, cc):
        cols = slice(c0, c0 + cc)
        for b in range(1, SUBLANES):
            xs_ref[b - 1] = ubuf[b:b + n_shift_rows, cols]
        for r0 in range(0, ts, CONV_ROW_CHUNK):
            acc = jnp.zeros((CONV_ROW_CHUNK, cc), F32)
            for k in range(ka):
                a, b = divmod(k + pa - (ka - 1), SUBLANES)
                start = a * SUBLANES + r0
                if b == 0:
                    src = ubuf[start:start + CONV_ROW_CHUNK, cols]
                else:
                    src = xs_ref[b - 1, start:start + CONV_ROW_CHUNK, :]
                acc = acc + dwa_ref[k:k + 1, cols] * src
            cbuf[r0:r0 + CONV_ROW_CHUNK, cols] = acc + ba_ref[:, cols]
        gb_ref[:, cols] = proj(2, c0)
        zbuf[pb:, cols] = proj(3, c0) * proj(4, c0)

    for r0 in range(0, ts, ROW_CHUNK):
        rows = slice(r0, r0 + ROW_CHUNK)
        ya = cbuf[rows, :]
        yc = ya - jnp.mean(ya, axis=-1, keepdims=True)
        yn = yc * _inv_rms(yc) * lng_ref[...] + lnb_ref[...]
        y_ref[rows, 0:c] = _silu(yn).astype(BF16)
        acc = jnp.zeros((ROW_CHUNK, c), F32)
        for k in range(kb):
            start = r0 + k + pb - (kb - 1)
            acc = acc + dwb_ref[k:k + 1, :] * zbuf[start:start + ROW_CHUNK, :]
        y_ref[rows, c:2 * c] = (gb_ref[rows, :] * acc).astype(BF16)

    o_ref[...] = _dot(y_ref[...], wout_ref[...])
    _gated_residual_rows(x_ref, gt_ref, gout_ref, o_ref, 1.0)

    @pl.when(s == pl.num_programs(1) - 1)
    def _():
        ca_ref[...] = ubuf[ts + pa - (ka - 1):ts + pa, :]
        cb_ref[...] = zbuf[ts + pb - (kb - 1):ts + pb, :]


def _conv_layer(x, mod, w_in, w_out, dw_a, b_a, ln_g, ln_b, dw_b, li, n, ts):
    m, d = x.shape
    s = m // n
    c = dw_a.shape[-1]
    ka, kb = dw_a.shape[1], dw_b.shape[1]
    pa = -(-(ka - 1) // SUBLANES) * SUBLANES
    pb = -(-(kb - 1) // SUBLANES) * SUBLANES
    cc = min(CONV_COL_CHUNK, c)
    nl = dw_a.shape[0]
    tile = pl.BlockSpec((None, ts, d), lambda b, t: (b, t, 0))
    vec = pl.BlockSpec((None, 1, c), lambda b, t: (li, 0, 0))
    resident = functools.partial(pl.BlockSpec, pipeline_mode=pl.Buffered(1))
    o, ca, cb = pl.pallas_call(
        functools.partial(_conv_layer_kernel, ka=ka, kb=kb),
        out_shape=[jax.ShapeDtypeStruct((n, s, d), F32),
                   jax.ShapeDtypeStruct((n, ka - 1, c), F32),
                   jax.ShapeDtypeStruct((n, kb - 1, c), F32)],
        grid=(n, s // ts),
        in_specs=[tile, mod.group_spec(0), mod.group_spec(1), mod.group_spec(2),
                  mod.gain_spec(0), mod.gain_spec(1),
                  resident((None, d, w_in.shape[-1]), lambda b, t: (0, 0, 0)),
                  pl.BlockSpec((None, ka, c), lambda b, t: (li, 0, 0)), vec, vec, vec,
                  pl.BlockSpec((None, kb, c), lambda b, t: (li, 0, 0)),
                  resident((None, 2 * c, d), lambda b, t: (0, 0, 0))],
        out_specs=[tile,
                   pl.BlockSpec((None, ka - 1, c), lambda b, t: (b, 0, 0)),
                   pl.BlockSpec((None, kb - 1, c), lambda b, t: (b, 0, 0))],
        scratch_shapes=[pltpu.VMEM((ts, d), BF16),
                        pltpu.VMEM((ts + pa, c), F32), pltpu.VMEM((ts + pb, c), F32),
                        pltpu.VMEM((ts, c), F32),
                        pltpu.VMEM((SUBLANES - 1, ts + pa - SUBLANES, cc), F32),
                        pltpu.VMEM((ts, c), F32), pltpu.VMEM((ts, 2 * c), BF16)],
        compiler_params=_params(2),
        name="conv_layer",
    )(x.reshape(n, s, d), mod.table, mod.table, mod.table, mod.norm_g4, mod.norm_g4, w_in,
      dw_a, b_a.reshape(nl, 1, c), ln_g.reshape(nl, 1, c), ln_b.reshape(nl, 1, c), dw_b, w_out)
    return o.reshape(m, d), ca, cb


def _multiplicity(dist, xp):
    cnt = 0
    for window, dil in DIL_PATTERNS:
        assert dil & (dil - 1) == 0
        hit = (dist >= 0) & (dist <= window) & ((dist & (dil - 1)) == 0)
        cnt = cnt + hit.astype(xp.int32)
    return cnt


def _log_multiplicity(dist):
    cnt = _multiplicity(dist, jnp)
    return jnp.where(cnt > 0, jnp.log(jnp.maximum(cnt, 1).astype(F32)), NEG)


def _attn_prompt_kernel(q_ref, k_ref, v_ref, slope_ref, o_ref,
                        vb_ref, s_ref, p_ref, bias_ref, logc_ref, dist_ref, *, tq):
    s_len, dh = q_ref.shape
    nq = s_len // tq

    @pl.when((pl.program_id(0) == 0) & (pl.program_id(1) == 0))
    def _():
        dist = (lax.broadcasted_iota(jnp.int32, (tq, s_len), 0) + (s_len - tq)
                - lax.broadcasted_iota(jnp.int32, (tq, s_len), 1))
        logc_ref[...] = _log_multiplicity(dist) * LOG2E
        dist_ref[...] = dist.astype(F32) * LOG2E
        vb_ref[:, dh:] = jnp.ones((s_len, dh), BF16)

    @pl.when(pl.program_id(1) == 0)
    def _():
        bias_ref[...] = logc_ref[...] - slope_ref[...] * dist_ref[...]

    vb_ref[:, :dh] = v_ref[...]

    def scores(i):
        c0 = (nq - 1 - i) * tq
        q = q_ref[i * tq:(i + 1) * tq, :]
        m = jnp.full((tq, 1), NEG, F32)
        for j in range(i + 1):
            keys = slice(j * tq, (j + 1) * tq)
            s = _dot_nt(q, k_ref[keys, :]) + bias_ref[:, c0 + j * tq:c0 + (j + 1) * tq]
            s_ref[i % 2, :, keys] = s
            m = jnp.maximum(m, jnp.max(s, axis=-1, keepdims=True))
        return m

    m_next = scores(0)
    for i in range(nq):
        slot, m = i % 2, m_next
        if i + 1 < nq:
            m_next = scores(i + 1)
        for j in range(i + 1):
            keys = slice(j * tq, (j + 1) * tq)
            p_ref[slot, :, keys] = jnp.exp2(s_ref[slot, :, keys] - m).astype(BF16)
        n_keys = (i + 1) * tq
        ol = _dot(p_ref[slot, :, :n_keys], vb_ref[:n_keys, :])
        o_ref[i * tq:(i + 1) * tq, :] = (ol[:, :dh] / ol[:, dh:]).astype(o_ref.dtype)


def _alibi_slopes(n_heads):
    return jnp.exp2(-8.0 * jnp.arange(1, n_heads + 1, dtype=F32) / n_heads)


def _attn_prompt(q, k, v, n, tq):
    m, d = q.shape
    s = m // n
    n_heads = d // HEAD_DIM
    slopes = jnp.broadcast_to(_alibi_slopes(n_heads)[:, None, None], (n_heads, 1, s))
    head = pl.BlockSpec((s, HEAD_DIM), lambda h, b: (b, h))
    return pl.pallas_call(
        functools.partial(_attn_prompt_kernel, tq=tq),
        out_shape=jax.ShapeDtypeStruct((m, d), BF16),
        grid=(n_heads, n),
        in_specs=[head, head, head, pl.BlockSpec((None, 1, s), lambda h, b: (h, 0, 0))],
        out_specs=head,
        scratch_shapes=[pltpu.VMEM((s, 2 * HEAD_DIM), BF16),
                        pltpu.VMEM((2, tq, s), F32), pltpu.VMEM((2, tq, s), BF16),
                        pltpu.VMEM((tq, s), F32), pltpu.VMEM((tq, s), F32), pltpu.VMEM((tq, s), F32)],
        compiler_params=_params(2),
        name="attn_prompt",
    )(q, k, v, slopes)


def _sample_regions(wb, t_new):
    dil_max = max(dil for _, dil in DIL_PATTERNS)
    reach = max(window for window, dil in DIL_PATTERNS if dil < dil_max)
    near = -(-reach // dil_max) * dil_max
    assert wb % dil_max == 0 and wb > near and t_new <= dil_max
    r = np.arange(wb - near)
    skipped = r[(r % dil_max) >= t_new]
    for t in range(t_new):
        assert not _multiplicity(wb + t - skipped, np).any()
    return dil_max, near


def _shift(x, n):
    assert n & (n - 1) == 0
    return x >> (n.bit_length() - 1), x & (n - 1)


def _attn_sample_kernel(q_ref, kfar_ref, vfar_ref, knear_ref, vnear_ref, knew_ref, vnew_ref,
                        o_ref, bfar_ref, bnear_ref, bnew_ref, *, t_new, n_heads, wb, dil_max):
    n_far = kfar_ref.shape[0] * kfar_ref.shape[1] * kfar_ref.shape[2]
    n_near = knear_ref.shape[0]
    dh = q_ref.shape[-1]

    @pl.when(pl.program_id(0) == 0)
    def _():
        def table(ref, row_of_slot):
            i = lax.broadcasted_iota(jnp.int32, ref.shape, 0)
            c = lax.broadcasted_iota(jnp.int32, ref.shape, 1)
            h, t = _shift(i, t_new)
            slot, hk = _shift(c, n_heads)
            dist = wb + t - row_of_slot(slot)
            slope = jnp.exp2(-8.0 * (h + 1).astype(F32) / n_heads)
            ref[...] = jnp.where(h == hk, _log_multiplicity(dist) - slope * dist.astype(F32), NEG)

        def far_row(slot):
            g, p = _shift(slot, t_new)
            return g * dil_max + p

        table(bfar_ref, far_row)
        table(bnear_ref, lambda slot: slot + (wb - n_near // n_heads))
        table(bnew_ref, lambda slot: slot + wb)

    q = (q_ref[...] * (dh ** -0.5)).astype(BF16)
    kfar = kfar_ref[...].reshape(n_far, dh).astype(BF16)
    s_far = _dot_nt(q, kfar) + bfar_ref[...]
    s_near = _dot_nt(q, knear_ref[...].astype(BF16)) + bnear_ref[...]
    s_new = _dot_nt(q, knew_ref[...].astype(BF16)) + bnew_ref[...]
    m = jnp.maximum(jnp.maximum(jnp.max(s_far, axis=-1, keepdims=True),
                                jnp.max(s_near, axis=-1, keepdims=True)),
                    jnp.max(s_new, axis=-1, keepdims=True))
    p_far = jnp.exp(s_far - m)
    p_near = jnp.exp(s_near - m)
    p_new = jnp.exp(s_new - m)
    l = (jnp.sum(p_far, axis=-1, keepdims=True) + jnp.sum(p_near, axis=-1, keepdims=True)
         + jnp.sum(p_new, axis=-1, keepdims=True))
    vfar = vfar_ref[...].reshape(n_far, dh).astype(BF16)
    o = (_dot(p_far.astype(BF16), vfar) + _dot(p_near.astype(BF16), vnear_ref[...].astype(BF16))
         + _dot(p_new.astype(BF16), vnew_ref[...].astype(BF16)))
    o_ref[...] = o / l


def _attn_sample(q, k_new, v_new, cache_k, cache_v, li, nb):
    m, d = q.shape
    t = m // nb
    n_layers, _, wb, n_heads, dh = cache_k.shape
    dil_max, near = _sample_regions(wb, t)
    n_groups = (wb - near) // dil_max
    by_head = lambda a: a.reshape(nb, t, n_heads, dh).transpose(0, 2, 1, 3).reshape(nb, n_heads * t, dh)
    rows = lambda a: a.reshape(nb, t * n_heads, dh)
    far = lambda c: c.reshape(n_layers, nb, wb // dil_max, dil_max, n_heads, dh)
    flat = lambda c: c.reshape(n_layers, nb, wb * n_heads, dh)
    qspec = pl.BlockSpec((None, n_heads * t, dh), lambda b: (b, 0, 0))
    far_spec = pl.BlockSpec((None, None, n_groups, t, n_heads, dh), lambda b: (li, b, 0, 0, 0, 0))
    near_spec = pl.BlockSpec((None, None, near * n_heads, dh), lambda b: (li, b, wb // near - 1, 0))
    assert wb % near == 0
    o = pl.pallas_call(
        functools.partial(_attn_sample_kernel, t_new=t, n_heads=n_heads, wb=wb, dil_max=dil_max),
        out_shape=jax.ShapeDtypeStruct((nb, n_heads * t, dh), F32),
        grid=(nb,),
        in_specs=[qspec, far_spec, far_spec, near_spec, near_spec, qspec, qspec],
        out_specs=qspec,
        scratch_shapes=[pltpu.VMEM((n_heads * t, n_groups * t * n_heads), F32),
                        pltpu.VMEM((n_heads * t, near * n_heads), F32),
                        pltpu.VMEM((n_heads * t, t * n_heads), F32)],
        compiler_params=_params(1),
        name="attn_sample",
    )(by_head(q), far(cache_k), far(cache_v), flat(cache_k), flat(cache_v), rows(k_new), rows(v_new))
    return o.reshape(nb, n_heads, t, dh).transpose(0, 2, 1, 3).reshape(m, d)


def _trunk_sample(x3, mod_table, p, caches, tiles):
    tm, tf = tiles["ffn_rows"], tiles["ffn_hidden"]
    n, s, d = x3.shape
    x = x3.reshape(n * s, d)
    depth = p["norm_g4"].shape[0]
    n_heads = d // HEAD_DIM
    ffn_w = (p["ffn_w_gate"], p["ffn_w_up"], p["ffn_w_down"])
    new_a, new_b, new_k, new_v = [], [], [], []
    rounded = {}
    for layer in range(depth):
        li = layer // 2
        mods = [_Mod(mod_table, True, layer, sub, p["norm_g4"]) for sub in range(3)]
        x, rounded["ffn", layer, 0] = _ffn(x, mods[0], ffn_w, layer, 0, s, tm, tf)
        if layer % 2 == 0:
            proj, rounded["conv_w_in", li] = _inproj_stream(x, mods[1], p["conv_w_in"], li, 5,
                                                            tiles["proj_cols"])
            y, ca, cb = _conv_mix(proj, caches[0], caches[1], p["conv_dw_a"], p["conv_b_a"],
                                  p["conv_ln_g"], p["conv_ln_b"], p["conv_dw_b"], li, n, s)
            new_a.append(ca)
            new_b.append(cb)
            x, rounded["conv_w_out", li] = _outproj_stream(y, x, mods[1], p["conv_w_out"], li,
                                                           tiles["proj_depth"])
        else:
            (q, k, v), rounded["attn_w_qkv", li] = _inproj_stream(x, mods[1], p["attn_w_qkv"], li, 3,
                                                                 tiles["proj_cols"])
            o = _attn_sample(q, k, v, caches[2], caches[3], li, n)
            new_k.append(k.reshape(n, s, n_heads, HEAD_DIM))
            new_v.append(v.reshape(n, s, n_heads, HEAD_DIM))
            x, rounded["attn_w_o", li] = _outproj_stream(o, x, mods[1], p["attn_w_o"], li,
                                                         tiles["proj_depth"])
        x, rounded["ffn", layer, 1] = _ffn(x, mods[2], ffn_w, layer, 1, s, tm, tf)
    outs = (x.reshape(n, s, d), jnp.stack(new_a), jnp.stack(new_b), jnp.stack(new_k), jnp.stack(new_v))
    return outs, rounded


def _trunk_prompt(x3, mod_table, p, rounded, tiles):
    tm, tf, tp = tiles["ffn_rows"], tiles["ffn_hidden"], tiles["proj_rows"]
    n, s, d = x3.shape
    x = x3.reshape(n * s, d)
    depth = p["norm_g4"].shape[0]
    n_heads = d // HEAD_DIM
    keep = min(DIL_PATTERNS[-1][0], s)
    new_a, new_b, new_k, new_v = [], [], [], []
    for layer in range(depth):
        li = layer // 2
        mods = [_Mod(mod_table, False, layer, sub, p["norm_g4"]) for sub in range(3)]
        x, _ = _ffn(x, mods[0], rounded["ffn", layer, 0], 0, 0, s, tm, tf)
        if layer % 2 == 0:
            x, ca, cb = _conv_layer(x, mods[1], rounded["conv_w_in", li], rounded["conv_w_out", li],
                                    p["conv_dw_a"], p["conv_b_a"], p["conv_ln_g"], p["conv_ln_b"],
                                    p["conv_dw_b"], li, n, tiles["conv_rows"])
            new_a.append(ca)
            new_b.append(cb)
        else:
            qb, kb, vb, k, v = _qkv_proj(x, mods[1], rounded["attn_w_qkv", li], 0, s, tp)
            o = _attn_prompt(qb, kb, vb, n, tiles["attn_rows"])
            new_k.append(k.reshape(n, s, n_heads, HEAD_DIM)[:, s - keep:])
            new_v.append(v.reshape(n, s, n_heads, HEAD_DIM)[:, s - keep:])
            x = _outproj(o, x, mods[1], rounded["attn_w_o", li], 0, s, tp)
        x, _ = _ffn(x, mods[2], rounded["ffn", layer, 1], 0, 0, s, tm, tf)
    return x.reshape(n, s, d), jnp.stack(new_a), jnp.stack(new_b), jnp.stack(new_k), jnp.stack(new_v)


def _tile(total, want):
    t = min(total, want)
    while total % t:
        t //= 2
    return t


def kernel(x_prompt, x_sample, cache_conv_a, cache_conv_b, cache_k, cache_v, c_prompt, c_sample,
           w_mod, b_mod, norm_g, ffn_w_gate, ffn_w_up, ffn_w_down, conv_w_in, conv_dw_a, conv_b_a,
           conv_ln_g, conv_ln_b, conv_dw_b, conv_w_out, attn_w_qkv, attn_w_o):
    n, s, d = x_prompt.shape
    nb, t, _ = x_sample.shape
    depth = w_mod.shape[0]

    mod = _mod_table(jnp.concatenate([c_prompt, c_sample], axis=0), w_mod, b_mod,
                     tn=_tile(w_mod.shape[-1], 1024))
    mod_p = mod[:, :n].reshape(depth, n, N_MOD, 1, d)
    mod_s = jnp.repeat(mod[:, n:].reshape(depth, nb, N_MOD, d), t, axis=1).transpose(0, 2, 1, 3)

    p = dict(
        norm_g4=norm_g.reshape(depth, 6, 1, d),
        ffn_w_gate=ffn_w_gate, ffn_w_up=ffn_w_up, ffn_w_down=ffn_w_down,
        conv_w_in=conv_w_in, conv_w_out=conv_w_out, attn_w_qkv=attn_w_qkv, attn_w_o=attn_w_o,
        conv_dw_a=conv_dw_a, conv_b_a=conv_b_a, conv_ln_g=conv_ln_g, conv_ln_b=conv_ln_b,
        conv_dw_b=conv_dw_b,
    )
    tf = _tile(ffn_w_gate.shape[-1], 512)
    tiles_prompt = dict(ffn_rows=_tile(s, 1024), ffn_hidden=tf, proj_rows=_tile(s, 256),
                        conv_rows=_tile(s, 256), attn_rows=_tile(s, 256))
    tiles_sample = dict(ffn_rows=nb * t, ffn_hidden=tf, proj_cols=1024, proj_depth=512)
    sample_out, rounded = _trunk_sample(x_sample, mod_s, p,
                                        (cache_conv_a, cache_conv_b, cache_k, cache_v), tiles_sample)
    sample_out, rounded, x_prompt, mod_p = lax.optimization_barrier((sample_out, rounded, x_prompt, mod_p))
    ys, as_, bs, ks, vs = sample_out
    yp, ap, bp, kp, vp = _trunk_prompt(x_prompt, mod_p, p, rounded, tiles_prompt)
    return (yp, ys, ap, as_, bp, bs, kp, vp, ks, vs)
```

```python
import functools

import numpy as np
import jax
import jax.numpy as jnp
from jax import lax
from jax.experimental import pallas as pl
from jax.experimental.pallas import tpu as pltpu

F32 = jnp.float32
BF16 = jnp.bfloat16

HEAD_DIM = 128
DIL_PATTERNS = ((128, 1), (512, 4), (2048, 16))
N_MOD = 9
EPS = 1e-6
NEG = -1e30
LOG2E = 1.4426950408889634

V7X_VMEM_BYTES = 64 * 1024 * 1024
VMEM_LIMIT_BYTES = V7X_VMEM_BYTES - 8 * 1024 * 1024
VMEM_LIMIT_SIDE_JOB_BYTES = V7X_VMEM_BYTES - 4 * 1024 * 1024
ROW_CHUNK = 16


def _params(n_axes, vmem_limit_bytes=VMEM_LIMIT_BYTES):
    return pltpu.CompilerParams(
        dimension_semantics=("arbitrary",) * n_axes,
        vmem_limit_bytes=vmem_limit_bytes,
    )


def _silu(x):
    return x * jax.nn.sigmoid(x)


def _dot(a, b):
    return jnp.dot(a, b, preferred_element_type=F32)


def _dot_nt(a, b):
    return lax.dot_general(a, b, (((1,), (1,)), ((), ())), preferred_element_type=F32)


def _inv_rms(x):
    return lax.rsqrt(jnp.mean(jnp.square(x), axis=-1, keepdims=True) + EPS)


def _mod_rows(ref, rows):
    return ref[rows, :] if ref.shape[0] > 1 else ref[...]


def _modulate_rows(x_ref, gin_ref, sc_ref, sh_ref, h_ref):
    per_row = sc_ref.shape[0] > 1
    gs = None if per_row else gin_ref[...] * (1.0 + sc_ref[...])
    for r0 in range(0, x_ref.shape[0], ROW_CHUNK):
        rows = slice(r0, r0 + ROW_CHUNK)
        if per_row:
            gs = gin_ref[...] * (1.0 + sc_ref[rows, :])
        h = x_ref[rows, :] * _inv_rms(x_ref[rows, :]) * gs + _mod_rows(sh_ref, rows)
        h_ref[rows, :] = h.astype(h_ref.dtype)


def _gated_residual_rows(x_ref, gt_ref, gout_ref, o_ref, weight):
    per_row = gt_ref.shape[0] > 1
    gs = None if per_row else weight * gt_ref[...] * gout_ref[...]
    for r0 in range(0, x_ref.shape[0], ROW_CHUNK):
        rows = slice(r0, r0 + ROW_CHUNK)
        if per_row:
            gs = weight * gt_ref[rows, :] * gout_ref[...]
        o_ref[rows, :] = x_ref[rows, :] + o_ref[rows, :] * _inv_rms(o_ref[rows, :]) * gs


def _mod_kernel(c_ref, w_ref, b_ref, o_ref):
    a = _silu(c_ref[...]).astype(BF16)
    o_ref[...] = _dot(a, w_ref[...].astype(BF16)) + b_ref[...]


def _mod_table(c_all, w_mod, b_mod, tn):
    n_layers, d, n9 = w_mod.shape
    r = c_all.shape[0]
    return pl.pallas_call(
        _mod_kernel,
        out_shape=jax.ShapeDtypeStruct((n_layers, r, n9), F32),
        grid=(n_layers, n9 // tn),
        in_specs=[
            pl.BlockSpec((r, d), lambda l, j: (0, 0)),
            pl.BlockSpec((None, d, tn), lambda l, j: (l, 0, j)),
            pl.BlockSpec((None, 1, tn), lambda l, j: (l, 0, j)),
        ],
        out_specs=pl.BlockSpec((None, r, tn), lambda l, j: (l, 0, j)),
        compiler_params=_params(2),
        name="mod_table",
    )(c_all, w_mod, b_mod.reshape(n_layers, 1, n9))


class _Mod:
    def __init__(self, table, per_row, layer, sub, norm_g4):
        self.table = table
        self.per_row = per_row
        self.layer = layer
        self.sub = sub
        self.norm_g4 = norm_g4

    def spec(self, which, tm, rows_per_group):
        l, idx = self.layer, 3 * self.sub + which
        d = self.table.shape[-1]
        if self.per_row:
            return pl.BlockSpec((None, None, tm, d), lambda i, *_: (l, idx, i, 0))
        return pl.BlockSpec((None, None, None, 1, d),
                            lambda i, *_: (l, (i * tm) // rows_per_group, idx, 0, 0))

    def whole_spec(self, which):
        assert self.per_row
        l, idx = self.layer, 3 * self.sub + which
        _, _, m, d = self.table.shape
        return pl.BlockSpec((None, None, m, d), lambda *_: (l, idx, 0, 0))

    def group_spec(self, which):
        assert not self.per_row
        l, idx = self.layer, 3 * self.sub + which
        d = self.table.shape[-1]
        return pl.BlockSpec((None, None, None, 1, d), lambda b, *_: (l, b, idx, 0, 0))

    def gain_spec(self, which):
        l, idx = self.layer, 2 * self.sub + which
        d = self.norm_g4.shape[-1]
        return pl.BlockSpec((None, None, 1, d), lambda i, *_: (l, idx, 0, 0))


def _ffn_kernel(x_ref, sh_ref, sc_ref, gt_ref, gin_ref, gout_ref, wg_ref, wu_ref, wd_ref, *rest,
                round_own, round_next):
    nxt, rest = (rest[:3], rest[3:]) if round_next else ((), rest)
    o_ref, rest = rest[0], rest[1:]
    *rounded, h_ref, a_ref = rest
    j = pl.program_id(1)
    last = pl.num_programs(1) - 1

    def hidden_tile(first):
        wg, wu, wd = wg_ref, wu_ref, wd_ref
        if round_own:
            for src, dst in zip((wg_ref, wu_ref, wd_ref), rounded):
                dst[...] = src[...].astype(BF16)
            wg, wu, wd = rounded
        if round_next:
            for src, dst in zip(nxt, rounded):
                dst[...] = src[...].astype(BF16)
        g = _dot(h_ref[...], wg[...])
        u = _dot(h_ref[...], wu[...])
        a_ref[...] = (_silu(g) * u).astype(BF16)
        if first:
            o_ref[...] = _dot(a_ref[...], wd[...])
        else:
            o_ref[...] += _dot(a_ref[...], wd[...])

    @pl.when(j == 0)
    def _():
        _modulate_rows(x_ref, gin_ref, sc_ref, sh_ref, h_ref)
        hidden_tile(True)

    @pl.when((j > 0) & (j < last))
    def _():
        hidden_tile(False)

    @pl.when(j == last)
    def _():
        hidden_tile(False)
        _gated_residual_rows(x_ref, gt_ref, gout_ref, o_ref, 0.5)


def _ffn(x, mod, weights, l, half, rows_per_group, tm, tf, next_f32=None):
    wg, wu, wd = weights
    m, d = x.shape
    f = wg.shape[-1]
    ni, nf = m // tm, f // tf
    round_own = wg.dtype != BF16
    round_next = next_f32 is not None
    assert not (round_own and round_next)
    assert not round_own or ni == 1
    assert nf >= 2
    row = pl.BlockSpec((tm, d), lambda i, j: (i, 0))
    in_specs = [
        row,
        mod.spec(0, tm, rows_per_group), mod.spec(1, tm, rows_per_group),
        mod.spec(2, tm, rows_per_group),
        mod.gain_spec(0), mod.gain_spec(1),
        pl.BlockSpec((None, None, d, tf), lambda i, j: (l, half, 0, j)),
        pl.BlockSpec((None, None, d, tf), lambda i, j: (l, half, 0, j)),
        pl.BlockSpec((None, None, tf, d), lambda i, j: (l, half, j, 0)),
    ]
    args = [x, mod.table, mod.table, mod.table, mod.norm_g4, mod.norm_g4, wg, wu, wd]
    out_shape = [jax.ShapeDtypeStruct((m, d), F32)]
    out_specs = [row]
    if round_own or round_next:
        out_shape += [jax.ShapeDtypeStruct((1, 1, d, f), BF16)] * 2 + [jax.ShapeDtypeStruct((1, 1, f, d), BF16)]
    if round_own:
        out_specs += [pl.BlockSpec((None, None, d, tf), lambda i, j: (0, 0, 0, j))] * 2
        out_specs += [pl.BlockSpec((None, None, tf, d), lambda i, j: (0, 0, j, 0))]
    if round_next:
        (ng, nu, nd), l2, half2 = next_f32
        db = d // ni
        assert db * ni == d and db % 128 == 0
        in_specs += [pl.BlockSpec((None, None, db, tf), lambda i, j: (l2, half2, i, j))] * 2
        in_specs += [pl.BlockSpec((None, None, tf, db), lambda i, j: (l2, half2, j, i))]
        args += [ng, nu, nd]
        out_specs += [pl.BlockSpec((None, None, db, tf), lambda i, j: (0, 0, i, j))] * 2
        out_specs += [pl.BlockSpec((None, None, tf, db), lambda i, j: (0, 0, j, i))]
    out = pl.pallas_call(
        functools.partial(_ffn_kernel, round_own=round_own, round_next=round_next),
        out_shape=out_shape,
        grid=(ni, nf),
        in_specs=in_specs,
        out_specs=out_specs,
        scratch_shapes=[pltpu.VMEM((tm, d), BF16), pltpu.VMEM((tm, tf), BF16)],
        compiler_params=_params(2, VMEM_LIMIT_SIDE_JOB_BYTES if round_next else VMEM_LIMIT_BYTES),
        name="ffn",
    )(*args)
    return (out[0], tuple(out[1:])) if len(out) > 1 else (out[0], None)


def _inproj_stream_kernel(x_ref, sh_ref, sc_ref, gin_ref, w_ref, o_ref, wb_ref, h_ref):
    @pl.when(pl.program_id(0) == 0)
    def _():
        _modulate_rows(x_ref, gin_ref, sc_ref, sh_ref, h_ref)

    wb_ref[...] = w_ref[...].astype(BF16)
    o_ref[...] = _dot(h_ref[...], wb_ref[...])


def _inproj_stream(x, mod, w, li, n_out, cols_per_step):
    m, d = x.shape
    n_cols = w.shape[-1]
    tn = _tile(n_cols, cols_per_step)
    whole = pl.BlockSpec((m, d), lambda j: (0, 0))
    o, wb = pl.pallas_call(
        _inproj_stream_kernel,
        out_shape=[jax.ShapeDtypeStruct((m, n_cols), F32), jax.ShapeDtypeStruct((1, d, n_cols), BF16)],
        grid=(n_cols // tn,),
        in_specs=[whole, mod.whole_spec(0), mod.whole_spec(1), mod.gain_spec(0),
                  pl.BlockSpec((None, d, tn), lambda j: (li, 0, j))],
        out_specs=[pl.BlockSpec((m, tn), lambda j: (0, j)),
                   pl.BlockSpec((None, d, tn), lambda j: (0, 0, j))],
        scratch_shapes=[pltpu.VMEM((m, d), BF16)],
        compiler_params=_params(1),
        name="inproj_stream",
    )(x, mod.table, mod.table, mod.norm_g4, w)
    return jnp.split(o, n_out, axis=1), wb


def _outproj_stream_kernel(y_ref, x_ref, gt_ref, gout_ref, w_ref, o_ref, wb_ref):
    j = pl.program_id(0)

    @pl.when(j == 0)
    def _():
        o_ref[...] = jnp.zeros(o_ref.shape, F32)

    wb_ref[...] = w_ref[...].astype(BF16)
    o_ref[...] += _dot(y_ref[...].astype(BF16), wb_ref[...])

    @pl.when(j == pl.num_programs(0) - 1)
    def _():
        _gated_residual_rows(x_ref, gt_ref, gout_ref, o_ref, 1.0)


def _outproj_stream(y, x, mod, w, li, depth_per_step):
    m, d = x.shape
    k = y.shape[-1]
    tk = _tile(k, depth_per_step)
    whole = pl.BlockSpec((m, d), lambda j: (0, 0))
    return pl.pallas_call(
        _outproj_stream_kernel,
        out_shape=[jax.ShapeDtypeStruct((m, d), F32), jax.ShapeDtypeStruct((1, k, d), BF16)],
        grid=(k // tk,),
        in_specs=[pl.BlockSpec((m, tk), lambda j: (0, j)), whole, mod.whole_spec(2), mod.gain_spec(1),
                  pl.BlockSpec((None, tk, d), lambda j: (li, j, 0))],
        out_specs=[whole, pl.BlockSpec((None, tk, d), lambda j: (0, j, 0))],
        compiler_params=_params(1),
        name="outproj_stream",
    )(y, x, mod.table, mod.norm_g4, w)


def _qkv_kernel(x_ref, sh_ref, sc_ref, gin_ref, w_ref, qb_ref, kb_ref, vb_ref, k_ref, v_ref, h_ref):
    _modulate_rows(x_ref, gin_ref, sc_ref, sh_ref, h_ref)
    d = k_ref.shape[-1]
    qb_ref[...] = (_dot(h_ref[...], w_ref[:, 0:d]) * (HEAD_DIM ** -0.5 * LOG2E)).astype(BF16)
    k_ref[...] = _dot(h_ref[...], w_ref[:, d:2 * d])
    kb_ref[...] = k_ref[...].astype(BF16)
    v_ref[...] = _dot(h_ref[...], w_ref[:, 2 * d:3 * d])
    vb_ref[...] = v_ref[...].astype(BF16)


def _qkv_proj(x, mod, w, li, rows_per_group, tm):
    m, d = x.shape
    out = pl.BlockSpec((tm, d), lambda i: (i, 0))
    return pl.pallas_call(
        _qkv_kernel,
        out_shape=[jax.ShapeDtypeStruct((m, d), BF16)] * 3 + [jax.ShapeDtypeStruct((m, d), F32)] * 2,
        grid=(m // tm,),
        in_specs=[
            pl.BlockSpec((tm, d), lambda i: (i, 0)),
            mod.spec(0, tm, rows_per_group), mod.spec(1, tm, rows_per_group),
            mod.gain_spec(0),
            pl.BlockSpec((None, d, 3 * d), lambda i: (li, 0, 0), pipeline_mode=pl.Buffered(1)),
        ],
        out_specs=[out] * 5,
        scratch_shapes=[pltpu.VMEM((tm, d), BF16)],
        compiler_params=_params(1),
        name="qkv_proj",
    )(x, mod.table, mod.table, mod.norm_g4, w)


def _outproj_kernel(y_ref, x_ref, gt_ref, gout_ref, w_ref, o_ref):
    o_ref[...] = _dot(y_ref[...].astype(BF16), w_ref[...])
    _gated_residual_rows(x_ref, gt_ref, gout_ref, o_ref, 1.0)


def _outproj(y, x, mod, w, li, rows_per_group, tm):
    m, d = x.shape
    k = y.shape[-1]
    return pl.pallas_call(
        _outproj_kernel,
        out_shape=jax.ShapeDtypeStruct((m, d), F32),
        grid=(m // tm,),
        in_specs=[
            pl.BlockSpec((tm, k), lambda i: (i, 0)),
            pl.BlockSpec((tm, d), lambda i: (i, 0)),
            mod.spec(2, tm, rows_per_group),
            mod.gain_spec(1),
            pl.BlockSpec((None, k, d), lambda i: (li, 0, 0), pipeline_mode=pl.Buffered(1)),
        ],
        out_specs=pl.BlockSpec((tm, d), lambda i: (i, 0)),
        compiler_params=_params(1),
        name="outproj",
    )(y, x, mod.table, mod.norm_g4, w)


CONV_ROW_CHUNK = 32
CONV_COL_CHUNK = 512


def _conv_kernel(av_ref, ag_ref, gb_ref, gc_ref, z_ref, ha_ref, hb_ref,
                 dwa_ref, ba_ref, lng_ref, lnb_ref, dwb_ref,
                 y_ref, ca_ref, cb_ref, ubuf, zbuf, cbuf, *, ka, kb):
    s = pl.program_id(1)
    ts, c = av_ref.shape
    pa = ubuf.shape[0] - ts
    pb = zbuf.shape[0] - ts

    @pl.when(s == 0)
    def _():
        ubuf[0:pa - (ka - 1), :] = jnp.zeros((pa - (ka - 1), c), F32)
        ubuf[pa - (ka - 1):pa, :] = ha_ref[...]
        zbuf[0:pb - (kb - 1), :] = jnp.zeros((pb - (kb - 1), c), F32)
        zbuf[pb - (kb - 1):pb, :] = hb_ref[...]

    @pl.when(s > 0)
    def _():
        ubuf[0:pa, :] = ubuf[ts:ts + pa, :]
        zbuf[0:pb, :] = zbuf[ts:ts + pb, :]

    ubuf[pa:, :] = av_ref[...] * jax.nn.sigmoid(ag_ref[...])
    zbuf[pb:, :] = gc_ref[...] * z_ref[...]

    rc = min(CONV_ROW_CHUNK, ts)
    cc = min(CONV_COL_CHUNK, c)
    for r0 in range(0, ts, rc):
        for c0 in range(0, c, cc):
            acc = jnp.zeros((rc, cc), F32)
            for k in range(ka):
                start = r0 + k + pa - (ka - 1)
                acc = acc + dwa_ref[k:k + 1, c0:c0 + cc] * ubuf[start:start + rc, c0:c0 + cc]
            cbuf[r0:r0 + rc, c0:c0 + cc] = acc
    ya = cbuf[...] + ba_ref[...]
    mu = jnp.mean(ya, axis=-1, keepdims=True)
    yc = ya - mu
    ya = yc * lax.rsqrt(jnp.mean(yc * yc, axis=-1, keepdims=True) + EPS) * lng_ref[...] + lnb_ref[...]
    y_ref[:, 0:c] = _silu(ya).astype(y_ref.dtype)

    acc = jnp.zeros((ts, c), F32)
    for k in range(kb):
        start = k + pb - (kb - 1)
        acc = acc + dwb_ref[k:k + 1, :] * zbuf[start:start + ts, :]
    y_ref[:, c:2 * c] = (gb_ref[...] * acc).astype(y_ref.dtype)

    @pl.when(s == pl.num_programs(1) - 1)
    def _():
        ca_ref[...] = ubuf[ts + pa - (ka - 1):ts + pa, :]
        cb_ref[...] = zbuf[ts + pb - (kb - 1):ts + pb, :]


def _conv_mix(proj, hist_a, hist_b, dw_a, b_a, ln_g, ln_b, dw_b, li, n, ts):
    c = proj[0].shape[-1]
    s = proj[0].shape[0] // n
    ka, kb = dw_a.shape[1], dw_b.shape[1]
    pa = -(-(ka - 1) // SUBLANES) * SUBLANES
    pb = -(-(kb - 1) // SUBLANES) * SUBLANES
    tile = pl.BlockSpec((None, ts, c), lambda b, t: (b, t, 0))
    vec = pl.BlockSpec((None, 1, c), lambda b, t: (li, 0, 0))
    in_specs = [tile] * 5
    args = [p.reshape(n, s, c) for p in proj]
    in_specs += [pl.BlockSpec((None, None, ka - 1, c), lambda b, t: (li, b, 0, 0)),
                 pl.BlockSpec((None, None, kb - 1, c), lambda b, t: (li, b, 0, 0))]
    args += [hist_a, hist_b]
    in_specs += [pl.BlockSpec((None, ka, c), lambda b, t: (li, 0, 0)), vec, vec, vec,
                 pl.BlockSpec((None, kb, c), lambda b, t: (li, 0, 0))]
    nl = dw_a.shape[0]
    args += [dw_a, b_a.reshape(nl, 1, c), ln_g.reshape(nl, 1, c), ln_b.reshape(nl, 1, c), dw_b]
    y, ca, cb = pl.pallas_call(
        functools.partial(_conv_kernel, ka=ka, kb=kb),
        out_shape=[jax.ShapeDtypeStruct((n, s, 2 * c), F32),
                   jax.ShapeDtypeStruct((n, ka - 1, c), F32),
                   jax.ShapeDtypeStruct((n, kb - 1, c), F32)],
        grid=(n, s // ts),
        in_specs=in_specs,
        out_specs=[pl.BlockSpec((None, ts, 2 * c), lambda b, t: (b, t, 0)),
                   pl.BlockSpec((None, ka - 1, c), lambda b, t: (b, 0, 0)),
                   pl.BlockSpec((None, kb - 1, c), lambda b, t: (b, 0, 0))],
        scratch_shapes=[pltpu.VMEM((ts + pa, c), F32), pltpu.VMEM((ts + pb, c), F32),
                        pltpu.VMEM((ts, c), F32)],
        compiler_params=_params(2),
        name="conv_mix",
    )(*args)
    return y.reshape(n * s, 2 * c), ca, cb


SUBLANES = 8


def _conv_layer_kernel(x_ref, sh_ref, sc_ref, gt_ref, gin_ref, gout_ref, win_ref,
                       dwa_ref, ba_ref, lng_ref, lnb_ref, dwb_ref, wout_ref,
                       o_ref, ca_ref, cb_ref,
                       h_ref, ubuf, zbuf, gb_ref, xs_ref, cbuf, y_ref, *, ka, kb):
    s = pl.program_id(1)
    ts = x_ref.shape[0]
    c = ubuf.shape[1]
    pa = ubuf.shape[0] - ts
    pb = zbuf.shape[0] - ts
    cc = xs_ref.shape[2]
    n_shift_rows = xs_ref.shape[1]

    @pl.when(s == 0)
    def _():
        ubuf[0:pa, :] = jnp.zeros((pa, c), F32)
        zbuf[0:pb, :] = jnp.zeros((pb, c), F32)

    @pl.when(s > 0)
    def _():
        ubuf[0:pa, :] = ubuf[ts:ts + pa, :]
        zbuf[0:pb, :] = zbuf[ts:ts + pb, :]

    _modulate_rows(x_ref, gin_ref, sc_ref, sh_ref, h_ref)

    def proj(part, c0):
        return _dot(h_ref[...], win_ref[:, part * c + c0:part * c + c0 + cc])

    for c0 in range(0, c, cc):
        ubuf[pa:, c0:c0 + cc] = proj(0, c0) * jax.nn.sigmoid(proj(1, c0))

    for c0 in range(0, c, cc):
        cols = slice(c0, c0 + cc)
        for b in range(1, SUBLANES):
            xs_ref[b - 1] = ubuf[b:b + n_shift_rows, cols]
        for r0 in range(0, ts, CONV_ROW_CHUNK):
            acc = jnp.zeros((CONV_ROW_CHUNK, cc), F32)
            for k in range(ka):
                a, b = divmod(k + pa - (ka - 1), SUBLANES)
                start = a * SUBLANES + r0
                if b == 0:
                    src = ubuf[start:start + CONV_ROW_CHUNK, cols]
                else:
                    src = xs_ref[b - 1, start:start + CONV_ROW_CHUNK, :]
                acc = acc + dwa_ref[k:k + 1, cols] * src
            cbuf[r0:r0 + CONV_ROW_CHUNK, cols] = acc + ba_ref[:, cols]
        gb_ref[:, cols] = proj(2, c0)
        zbuf[pb:, cols] = proj(3, c0) * proj(4, c0)

    for r0 in range(0, ts, ROW_CHUNK):
        rows = slice(r0, r0 + ROW_CHUNK)
        ya = cbuf[rows, :]
        yc = ya - jnp.mean(ya, axis=-1, keepdims=True)
        yn = yc * _inv_rms(yc) * lng_ref[...] + lnb_ref[...]
        y_ref[rows, 0:c] = _silu(yn).astype(BF16)
        acc = jnp.zeros((ROW_CHUNK, c), F32)
        for k in range(kb):
            start = r0 + k + pb - (kb - 1)
            acc = acc + dwb_ref[k:k + 1, :] * zbuf[start:start + ROW_CHUNK, :]
        y_ref[rows, c:2 * c] = (gb_ref[rows, :] * acc).astype(BF16)

    o_ref[...] = _dot(y_ref[...], wout_ref[...])
    _gated_residual_rows(x_ref, gt_ref, gout_ref, o_ref, 1.0)

    @pl.when(s == pl.num_programs(1) - 1)
    def _():
        ca_ref[...] = ubuf[ts + pa - (ka - 1):ts + pa, :]
        cb_ref[...] = zbuf[ts + pb - (kb - 1):ts + pb, :]


def _conv_layer(x, mod, w_in, w_out, dw_a, b_a, ln_g, ln_b, dw_b, li, n, ts):
    m, d = x.shape
    s = m // n
    c = dw_a.shape[-1]
    ka, kb = dw_a.shape[1], dw_b.shape[1]
    pa = -(-(ka - 1) // SUBLANES) * SUBLANES
    pb = -(-(kb - 1) // SUBLANES) * SUBLANES
    cc = min(CONV_COL_CHUNK, c)
    nl = dw_a.shape[0]
    tile = pl.BlockSpec((None, ts, d), lambda b, t: (b, t, 0))
    vec = pl.BlockSpec((None, 1, c), lambda b, t: (li, 0, 0))
    resident = functools.partial(pl.BlockSpec, pipeline_mode=pl.Buffered(1))
    o, ca, cb = pl.pallas_call(
        functools.partial(_conv_layer_kernel, ka=ka, kb=kb),
        out_shape=[jax.ShapeDtypeStruct((n, s, d), F32),
                   jax.ShapeDtypeStruct((n, ka - 1, c), F32),
                   jax.ShapeDtypeStruct((n, kb - 1, c), F32)],
        grid=(n, s // ts),
        in_specs=[tile, mod.group_spec(0), mod.group_spec(1), mod.group_spec(2),
                  mod.gain_spec(0), mod.gain_spec(1),
                  resident((None, d, w_in.shape[-1]), lambda b, t: (0, 0, 0)),
                  pl.BlockSpec((None, ka, c), lambda b, t: (li, 0, 0)), vec, vec, vec,
                  pl.BlockSpec((None, kb, c), lambda b, t: (li, 0, 0)),
                  resident((None, 2 * c, d), lambda b, t: (0, 0, 0))],
        out_specs=[tile,
                   pl.BlockSpec((None, ka - 1, c), lambda b, t: (b, 0, 0)),
                   pl.BlockSpec((None, kb - 1, c), lambda b, t: (b, 0, 0))],
        scratch_shapes=[pltpu.VMEM((ts, d), BF16),
                        pltpu.VMEM((ts + pa, c), F32), pltpu.VMEM((ts + pb, c), F32),
                        pltpu.VMEM((ts, c), F32),
                        pltpu.VMEM((SUBLANES - 1, ts + pa - SUBLANES, cc), F32),
                        pltpu.VMEM((ts, c), F32), pltpu.VMEM((ts, 2 * c), BF16)],
        compiler_params=_params(2),
        name="conv_layer",
    )(x.reshape(n, s, d), mod.table, mod.table, mod.table, mod.norm_g4, mod.norm_g4, w_in,
      dw_a, b_a.reshape(nl, 1, c), ln_g.reshape(nl, 1, c), ln_b.reshape(nl, 1, c), dw_b, w_out)
    return o.reshape(m, d), ca, cb


def _multiplicity(dist, xp):
    cnt = 0
    for window, dil in DIL_PATTERNS:
        assert dil & (dil - 1) == 0
        hit = (dist >= 0) & (dist <= window) & ((dist & (dil - 1)) == 0)
        cnt = cnt + hit.astype(xp.int32)
    return cnt


def _log_multiplicity(dist):
    cnt = _multiplicity(dist, jnp)
    return jnp.where(cnt > 0, jnp.log(jnp.maximum(cnt, 1).astype(F32)), NEG)


def _attn_prompt_kernel(q_ref, k_ref, v_ref, slope_ref, o_ref,
                        vb_ref, s_ref, p_ref, bias_ref, logc_ref, dist_ref, *, tq):
    s_len, dh = q_ref.shape
    nq = s_len // tq

    @pl.when((pl.program_id(0) == 0) & (pl.program_id(1) == 0))
    def _():
        dist = (lax.broadcasted_iota(jnp.int32, (tq, s_len), 0) + (s_len - tq)
                - lax.broadcasted_iota(jnp.int32, (tq, s_len), 1))
        logc_ref[...] = _log_multiplicity(dist) * LOG2E
        dist_ref[...] = dist.astype(F32) * LOG2E
        vb_ref[:, dh:] = jnp.ones((s_len, dh), BF16)

    @pl.when(pl.program_id(1) == 0)
    def _():
        bias_ref[...] = logc_ref[...] - slope_ref[...] * dist_ref[...]

    vb_ref[:, :dh] = v_ref[...]

    def scores(i):
        c0 = (nq - 1 - i) * tq
        q = q_ref[i * tq:(i + 1) * tq, :]
        m = jnp.full((tq, 1), NEG, F32)
        for j in range(i + 1):
            keys = slice(j * tq, (j + 1) * tq)
            s = _dot_nt(q, k_ref[keys, :]) + bias_ref[:, c0 + j * tq:c0 + (j + 1) * tq]
            s_ref[i % 2, :, keys] = s
            m = jnp.maximum(m, jnp.max(s, axis=-1, keepdims=True))
        return m

    m_next = scores(0)
    for i in range(nq):
        slot, m = i % 2, m_next
        if i + 1 < nq:
            m_next = scores(i + 1)
        for j in range(i + 1):
            keys = slice(j * tq, (j + 1) * tq)
            p_ref[slot, :, keys] = jnp.exp2(s_ref[slot, :, keys] - m).astype(BF16)
        n_keys = (i + 1) * tq
        ol = _dot(p_ref[slot, :, :n_keys], vb_ref[:n_keys, :])
        o_ref[i * tq:(i + 1) * tq, :] = (ol[:, :dh] / ol[:, dh:]).astype(o_ref.dtype)


def _alibi_slopes(n_heads):
    return jnp.exp2(-8.0 * jnp.arange(1, n_heads + 1, dtype=F32) / n_heads)


def _attn_prompt(q, k, v, n, tq):
    m, d = q.shape
    s = m // n
    n_heads = d // HEAD_DIM
    slopes = jnp.broadcast_to(_alibi_slopes(n_heads)[:, None, None], (n_heads, 1, s))
    head = pl.BlockSpec((s, HEAD_DIM), lambda h, b: (b, h))
    return pl.pallas_call(
        functools.partial(_attn_prompt_kernel, tq=tq),
        out_shape=jax.ShapeDtypeStruct((m, d), BF16),
        grid=(n_heads, n),
        in_specs=[head, head, head, pl.BlockSpec((None, 1, s), lambda h, b: (h, 0, 0))],
        out_specs=head,
        scratch_shapes=[pltpu.VMEM((s, 2 * HEAD_DIM), BF16),
                        pltpu.VMEM((2, tq, s), F32), pltpu.VMEM((2, tq, s), BF16),
                        pltpu.VMEM((tq, s), F32), pltpu.VMEM((tq, s), F32), pltpu.VMEM((tq, s), F32)],
        compiler_params=_params(2),
        name="attn_prompt",
    )(q, k, v, slopes)


def _sample_regions(wb, t_new):
    dil_max = max(dil for _, dil in DIL_PATTERNS)
    reach = max(window for window, dil in DIL_PATTERNS if dil < dil_max)
    near = -(-reach // dil_max) * dil_max
    assert wb % dil_max == 0 and wb > near and t_new <= dil_max
    r = np.arange(wb - near)
    skipped = r[(r % dil_max) >= t_new]
    for t in range(t_new):
        assert not _multiplicity(wb + t - skipped, np).any()
    return dil_max, near


def _shift(x, n):
    assert n & (n - 1) == 0
    return x >> (n.bit_length() - 1), x & (n - 1)


def _attn_sample_kernel(q_ref, kfar_ref, vfar_ref, knear_ref, vnear_ref, knew_ref, vnew_ref,
                        o_ref, bfar_ref, bnear_ref, bnew_ref, *, t_new, n_heads, wb, dil_max):
    n_far = kfar_ref.shape[0] * kfar_ref.shape[1] * kfar_ref.shape[2]
    n_near = knear_ref.shape[0]
    dh = q_ref.shape[-1]

    @pl.when(pl.program_id(0) == 0)
    def _():
        def table(ref, row_of_slot):
            i = lax.broadcasted_iota(jnp.int32, ref.shape, 0)
            c = lax.broadcasted_iota(jnp.int32, ref.shape, 1)
            h, t = _shift(i, t_new)
            slot, hk = _shift(c, n_heads)
            dist = wb + t - row_of_slot(slot)
            slope = jnp.exp2(-8.0 * (h + 1).astype(F32) / n_heads)
            ref[...] = jnp.where(h == hk, _log_multiplicity(dist) - slope * dist.astype(F32), NEG)

        def far_row(slot):
            g, p = _shift(slot, t_new)
            return g * dil_max + p

        table(bfar_ref, far_row)
        table(bnear_ref, lambda slot: slot + (wb - n_near // n_heads))
        table(bnew_ref, lambda slot: slot + wb)

    q = (q_ref[...] * (dh ** -0.5)).astype(BF16)
    kfar = kfar_ref[...].reshape(n_far, dh).astype(BF16)
    s_far = _dot_nt(q, kfar) + bfar_ref[...]
    s_near = _dot_nt(q, knear_ref[...].astype(BF16)) + bnear_ref[...]
    s_new = _dot_nt(q, knew_ref[...].astype(BF16)) + bnew_ref[...]
    m = jnp.maximum(jnp.maximum(jnp.max(s_far, axis=-1, keepdims=True),
                                jnp.max(s_near, axis=-1, keepdims=True)),
                    jnp.max(s_new, axis=-1, keepdims=True))
    p_far = jnp.exp(s_far - m)
    p_near = jnp.exp(s_near - m)
    p_new = jnp.exp(s_new - m)
    l = (jnp.sum(p_far, axis=-1, keepdims=True) + jnp.sum(p_near, axis=-1, keepdims=True)
         + jnp.sum(p_new, axis=-1, keepdims=True))
    vfar = vfar_ref[...].reshape(n_far, dh).astype(BF16)
    o = (_dot(p_far.astype(BF16), vfar) + _dot(p_near.astype(BF16), vnear_ref[...].astype(BF16))
         + _dot(p_new.astype(BF16), vnew_ref[...].astype(BF16)))
    o_ref[...] = o / l


def _attn_sample(q, k_new, v_new, cache_k, cache_v, li, nb):
    m, d = q.shape
    t = m // nb
    n_layers, _, wb, n_heads, dh = cache_k.shape
    dil_max, near = _sample_regions(wb, t)
    n_groups = (wb - near) // dil_max
    by_head = lambda a: a.reshape(nb, t, n_heads, dh).transpose(0, 2, 1, 3).reshape(nb, n_heads * t, dh)
    rows = lambda a: a.reshape(nb, t * n_heads, dh)
    far = lambda c: c.reshape(n_layers, nb, wb // dil_max, dil_max, n_heads, dh)
    flat = lambda c: c.reshape(n_layers, nb, wb * n_heads, dh)
    qspec = pl.BlockSpec((None, n_heads * t, dh), lambda b: (b, 0, 0))
    far_spec = pl.BlockSpec((None, None, n_groups, t, n_heads, dh), lambda b: (li, b, 0, 0, 0, 0))
    near_spec = pl.BlockSpec((None, None, near * n_heads, dh), lambda b: (li, b, wb // near - 1, 0))
    assert wb % near == 0
    o = pl.pallas_call(
        functools.partial(_attn_sample_kernel, t_new=t, n_heads=n_heads, wb=wb, dil_max=dil_max),
        out_shape=jax.ShapeDtypeStruct((nb, n_heads * t, dh), F32),
        grid=(nb,),
        in_specs=[qspec, far_spec, far_spec, near_spec, near_spec, qspec, qspec],
        out_specs=qspec,
        scratch_shapes=[pltpu.VMEM((n_heads * t, n_groups * t * n_heads), F32),
                        pltpu.VMEM((n_heads * t, near * n_heads), F32),
                        pltpu.VMEM((n_heads * t, t * n_heads), F32)],
        compiler_params=_params(1),
        name="attn_sample",
    )(by_head(q), far(cache_k), far(cache_v), flat(cache_k), flat(cache_v), rows(k_new), rows(v_new))
    return o.reshape(nb, n_heads, t, dh).transpose(0, 2, 1, 3).reshape(m, d)


def _ffn_weights(p):
    return (p["ffn_w_gate"], p["ffn_w_up"], p["ffn_w_down"])


def _walk_sample(st, mod_table, p, rounded, caches, tiles):
    tm, tf = tiles["ffn_rows"], tiles["ffn_hidden"]
    n, s, d = st["shape"]
    depth = p["norm_g4"].shape[0]
    n_heads = d // HEAD_DIM
    new_a, new_b, new_k, new_v = [], [], [], []
    for layer in range(depth):
        li = layer // 2
        mods = [_Mod(mod_table, True, layer, sub, p["norm_g4"]) for sub in range(3)]
        if layer == 0:
            st["x"], rounded["ffn", 0] = _ffn(st["x"], mods[0], _ffn_weights(p), 0, 0, s, tm, tf)
        else:
            st["x"], _ = _ffn(st["x"], mods[0], rounded["ffn", 2 * layer], 0, 0, s, tm, tf)
        yield
        x = st["x"]
        if layer % 2 == 0:
            proj, rounded["conv_w_in", li] = _inproj_stream(x, mods[1], p["conv_w_in"], li, 5,
                                                            tiles["proj_cols"])
            y, ca, cb = _conv_mix(proj, caches[0], caches[1], p["conv_dw_a"], p["conv_b_a"],
                                  p["conv_ln_g"], p["conv_ln_b"], p["conv_dw_b"], li, n, s)
            new_a.append(ca)
            new_b.append(cb)
            x, rounded["conv_w_out", li] = _outproj_stream(y, x, mods[1], p["conv_w_out"], li,
                                                           tiles["proj_depth"])
        else:
            (q, k, v), rounded["attn_w_qkv", li] = _inproj_stream(x, mods[1], p["attn_w_qkv"], li, 3,
                                                                 tiles["proj_cols"])
            o = _attn_sample(q, k, v, caches[2], caches[3], li, n)
            new_k.append(k.reshape(n, s, n_heads, HEAD_DIM))
            new_v.append(v.reshape(n, s, n_heads, HEAD_DIM))
            x, rounded["attn_w_o", li] = _outproj_stream(o, x, mods[1], p["attn_w_o"], li,
                                                         tiles["proj_depth"])
        st["x"] = x
        yield
        st["x"], _ = _ffn(st["x"], mods[2], rounded["ffn", 2 * layer + 1], 0, 0, s, tm, tf)
        if layer == depth - 1:
            st["out"] = (st["x"].reshape(n, s, d), jnp.stack(new_a), jnp.stack(new_b),
                         jnp.stack(new_k), jnp.stack(new_v))
        yield


def _walk_prompt(st, mod_table, p, rounded, tiles):
    tm, tf, tp = tiles["ffn_rows"], tiles["ffn_hidden"], tiles["proj_rows"]
    n, s, d = st["shape"]
    depth = p["norm_g4"].shape[0]
    n_heads = d // HEAD_DIM
    keep = min(DIL_PATTERNS[-1][0], s)
    new_a, new_b, new_k, new_v = [], [], [], []

    def ffn(k, mod):
        nxt = (_ffn_weights(p), (k + 1) // 2, (k + 1) % 2) if k + 1 < 2 * depth else None
        st["x"], copies = _ffn(st["x"], mod, rounded["ffn", k], 0, 0, s, tm, tf, next_f32=nxt)
        if nxt is not None:
            rounded["ffn", k + 1] = copies

    for layer in range(depth):
        li = layer // 2
        mods = [_Mod(mod_table, False, layer, sub, p["norm_g4"]) for sub in range(3)]
        ffn(2 * layer, mods[0])
        yield
        x = st["x"]
        if layer % 2 == 0:
            x, ca, cb = _conv_layer(x, mods[1], rounded["conv_w_in", li], rounded["conv_w_out", li],
                                    p["conv_dw_a"], p["conv_b_a"], p["conv_ln_g"], p["conv_ln_b"],
                                    p["conv_dw_b"], li, n, tiles["conv_rows"])
            new_a.append(ca)
            new_b.append(cb)
        else:
            qb, kb, vb, k, v = _qkv_proj(x, mods[1], rounded["attn_w_qkv", li], 0, s, tp)
            o = _attn_prompt(qb, kb, vb, n, tiles["attn_rows"])
            new_k.append(k.reshape(n, s, n_heads, HEAD_DIM)[:, s - keep:])
            new_v.append(v.reshape(n, s, n_heads, HEAD_DIM)[:, s - keep:])
            x = _outproj(o, x, mods[1], rounded["attn_w_o", li], 0, s, tp)
        st["x"] = x
        yield
        ffn(2 * layer + 1, mods[2])
        if layer == depth - 1:
            st["out"] = (st["x"].reshape(n, s, d), jnp.stack(new_a), jnp.stack(new_b),
                         jnp.stack(new_k), jnp.stack(new_v))
        yield


def _tile(total, want):
    t = min(total, want)
    while total % t:
        t //= 2
    return t


def kernel(x_prompt, x_sample, cache_conv_a, cache_conv_b, cache_k, cache_v, c_prompt, c_sample,
           w_mod, b_mod, norm_g, ffn_w_gate, ffn_w_up, ffn_w_down, conv_w_in, conv_dw_a, conv_b_a,
           conv_ln_g, conv_ln_b, conv_dw_b, conv_w_out, attn_w_qkv, attn_w_o):
    n, s, d = x_prompt.shape
    nb, t, _ = x_sample.shape
    depth = w_mod.shape[0]

    mod = _mod_table(jnp.concatenate([c_prompt, c_sample], axis=0), w_mod, b_mod,
                     tn=_tile(w_mod.shape[-1], 1024))
    mod_p = mod[:, :n].reshape(depth, n, N_MOD, 1, d)
    mod_s = jnp.repeat(mod[:, n:].reshape(depth, nb, N_MOD, d), t, axis=1).transpose(0, 2, 1, 3)

    p = dict(
        norm_g4=norm_g.reshape(depth, 6, 1, d),
        ffn_w_gate=ffn_w_gate, ffn_w_up=ffn_w_up, ffn_w_down=ffn_w_down,
        conv_w_in=conv_w_in, conv_w_out=conv_w_out, attn_w_qkv=attn_w_qkv, attn_w_o=attn_w_o,
        conv_dw_a=conv_dw_a, conv_b_a=conv_b_a, conv_ln_g=conv_ln_g, conv_ln_b=conv_ln_b,
        conv_dw_b=conv_dw_b,
    )
    tf = _tile(ffn_w_gate.shape[-1], 512)
    tiles_prompt = dict(ffn_rows=_tile(s, 1024), ffn_hidden=tf, proj_rows=_tile(s, 256),
                        conv_rows=_tile(s, 256), attn_rows=_tile(s, 256))
    tiles_sample = dict(ffn_rows=nb * t, ffn_hidden=tf, proj_cols=1024, proj_depth=512)
    rounded = {}
    st_s = dict(x=x_sample.reshape(nb * t, d), shape=(nb, t, d))
    st_p = dict(x=x_prompt.reshape(n * s, d), shape=(n, s, d))
    walk_s = _walk_sample(st_s, mod_s, p, rounded, (cache_conv_a, cache_conv_b, cache_k, cache_v), tiles_sample)
    walk_p = _walk_prompt(st_p, mod_p, p, rounded, tiles_prompt)
    for step in range(3 * depth):
        next(walk_s)
        if step % 3 == 1 and (step // 3) % 2 == 1:
            st_s["x"], st_p["x"] = lax.optimization_barrier((st_s["x"], st_p["x"]))
        next(walk_p)
    ys, as_, bs, ks, vs = st_s["out"]
    yp, ap, bp, kp, vp = st_p["out"]
    return (yp, ys, ap, as_, bp, bs, kp, vp, ks, vs)
```

```python
import functools

import numpy as np
import jax
import jax.numpy as jnp
from jax import lax
from jax.experimental import pallas as pl
from jax.experimental.pallas import tpu as pltpu

F32 = jnp.float32
BF16 = jnp.bfloat16

HEAD_DIM = 128
DIL_PATTERNS = ((128, 1), (512, 4), (2048, 16))
N_MOD = 9
EPS = 1e-6
NEG = -1e30
LOG2E = 1.4426950408889634

V7X_VMEM_BYTES = 64 * 1024 * 1024
VMEM_LIMIT_BYTES = V7X_VMEM_BYTES - 8 * 1024 * 1024
ROW_CHUNK = 16


def _params(n_axes):
    return pltpu.CompilerParams(
        dimension_semantics=("arbitrary",) * n_axes,
        vmem_limit_bytes=VMEM_LIMIT_BYTES,
    )


def _silu(x):
    return x * jax.nn.sigmoid(x)


def _dot(a, b):
    return jnp.dot(a, b, preferred_element_type=F32)


def _dot_nt(a, b):
    return lax.dot_general(a, b, (((1,), (1,)), ((), ())), preferred_element_type=F32)


def _inv_rms(x):
    return lax.rsqrt(jnp.mean(jnp.square(x), axis=-1, keepdims=True) + EPS)


def _mod_rows(ref, rows):
    return ref[rows, :] if ref.shape[0] > 1 else ref[...]


def _modulate_rows(x_ref, gin_ref, sc_ref, sh_ref, h_ref):
    per_row = sc_ref.shape[0] > 1
    gs = None if per_row else gin_ref[...] * (1.0 + sc_ref[...])
    for r0 in range(0, x_ref.shape[0], ROW_CHUNK):
        rows = slice(r0, r0 + ROW_CHUNK)
        if per_row:
            gs = gin_ref[...] * (1.0 + sc_ref[rows, :])
        h = x_ref[rows, :] * _inv_rms(x_ref[rows, :]) * gs + _mod_rows(sh_ref, rows)
        h_ref[rows, :] = h.astype(h_ref.dtype)


def _gated_residual_rows(x_ref, gt_ref, gout_ref, o_ref, weight):
    per_row = gt_ref.shape[0] > 1
    gs = None if per_row else weight * gt_ref[...] * gout_ref[...]
    for r0 in range(0, x_ref.shape[0], ROW_CHUNK):
        rows = slice(r0, r0 + ROW_CHUNK)
        if per_row:
            gs = weight * gt_ref[rows, :] * gout_ref[...]
        o_ref[rows, :] = x_ref[rows, :] + o_ref[rows, :] * _inv_rms(o_ref[rows, :]) * gs


def _mod_kernel(c_ref, w_ref, b_ref, o_ref):
    a = _silu(c_ref[...]).astype(BF16)
    o_ref[...] = _dot(a, w_ref[...].astype(BF16)) + b_ref[...]


def _mod_table(c_all, w_mod, b_mod, tn):
    n_layers, d, n9 = w_mod.shape
    r = c_all.shape[0]
    return pl.pallas_call(
        _mod_kernel,
        out_shape=jax.ShapeDtypeStruct((n_layers, r, n9), F32),
        grid=(n_layers, n9 // tn),
        in_specs=[
            pl.BlockSpec((r, d), lambda l, j: (0, 0)),
            pl.BlockSpec((None, d, tn), lambda l, j: (l, 0, j)),
            pl.BlockSpec((None, 1, tn), lambda l, j: (l, 0, j)),
        ],
        out_specs=pl.BlockSpec((None, r, tn), lambda l, j: (l, 0, j)),
        compiler_params=_params(2),
        name="mod_table",
    )(c_all, w_mod, b_mod.reshape(n_layers, 1, n9))


class _Mod:
    def __init__(self, table, per_row, layer, sub, norm_g4):
        self.table = table
        self.per_row = per_row
        self.layer = layer
        self.sub = sub
        self.norm_g4 = norm_g4

    def spec(self, which, tm, rows_per_group):
        l, idx = self.layer, 3 * self.sub + which
        d = self.table.shape[-1]
        if self.per_row:
            return pl.BlockSpec((None, None, tm, d), lambda i, *_: (l, idx, i, 0))
        return pl.BlockSpec((None, None, None, 1, d),
                            lambda i, *_: (l, (i * tm) // rows_per_group, idx, 0, 0))

    def whole_spec(self, which):
        assert self.per_row
        l, idx = self.layer, 3 * self.sub + which
        _, _, m, d = self.table.shape
        return pl.BlockSpec((None, None, m, d), lambda *_: (l, idx, 0, 0))

    def group_spec(self, which):
        assert not self.per_row
        l, idx = self.layer, 3 * self.sub + which
        d = self.table.shape[-1]
        return pl.BlockSpec((None, None, None, 1, d), lambda b, *_: (l, b, idx, 0, 0))

    def gain_spec(self, which):
        l, idx = self.layer, 2 * self.sub + which
        d = self.norm_g4.shape[-1]
        return pl.BlockSpec((None, None, 1, d), lambda i, *_: (l, idx, 0, 0))


def _ffn_kernel(x_ref, sh_ref, sc_ref, gt_ref, gin_ref, gout_ref, wg_ref, wu_ref, wd_ref,
                o_ref, *rest):
    *rounded, h_ref, a_ref = rest
    j = pl.program_id(1)
    last = pl.num_programs(1) - 1

    def hidden_tile(first):
        wg, wu, wd = wg_ref, wu_ref, wd_ref
        if rounded:
            for src, dst in zip((wg_ref, wu_ref, wd_ref), rounded):
                dst[...] = src[...].astype(BF16)
            wg, wu, wd = rounded
        g = _dot(h_ref[...], wg[...])
        u = _dot(h_ref[...], wu[...])
        a_ref[...] = (_silu(g) * u).astype(BF16)
        if first:
            o_ref[...] = _dot(a_ref[...], wd[...])
        else:
            o_ref[...] += _dot(a_ref[...], wd[...])

    @pl.when(j == 0)
    def _():
        _modulate_rows(x_ref, gin_ref, sc_ref, sh_ref, h_ref)
        hidden_tile(True)

    @pl.when((j > 0) & (j < last))
    def _():
        hidden_tile(False)

    @pl.when(j == last)
    def _():
        hidden_tile(False)
        _gated_residual_rows(x_ref, gt_ref, gout_ref, o_ref, 0.5)


def _ffn(x, mod, weights, l, half, rows_per_group, tm, tf):
    wg, wu, wd = weights
    m, d = x.shape
    f = wg.shape[-1]
    emit = wg.dtype != BF16
    assert not emit or m == tm
    assert f // tf >= 2
    row = pl.BlockSpec((tm, d), lambda i, j: (i, 0))
    out_shape = [jax.ShapeDtypeStruct((m, d), F32)]
    out_specs = [row]
    if emit:
        out_shape += [jax.ShapeDtypeStruct((1, 1, d, f), BF16)] * 2 + [jax.ShapeDtypeStruct((1, 1, f, d), BF16)]
        out_specs += [pl.BlockSpec((None, None, d, tf), lambda i, j: (0, 0, 0, j))] * 2
        out_specs += [pl.BlockSpec((None, None, tf, d), lambda i, j: (0, 0, j, 0))]
    out = pl.pallas_call(
        _ffn_kernel,
        out_shape=out_shape,
        grid=(m // tm, f // tf),
        in_specs=[
            row,
            mod.spec(0, tm, rows_per_group), mod.spec(1, tm, rows_per_group),
            mod.spec(2, tm, rows_per_group),
            mod.gain_spec(0), mod.gain_spec(1),
            pl.BlockSpec((None, None, d, tf), lambda i, j: (l, half, 0, j)),
            pl.BlockSpec((None, None, d, tf), lambda i, j: (l, half, 0, j)),
            pl.BlockSpec((None, None, tf, d), lambda i, j: (l, half, j, 0)),
        ],
        out_specs=out_specs,
        scratch_shapes=[pltpu.VMEM((tm, d), BF16), pltpu.VMEM((tm, tf), BF16)],
        compiler_params=_params(2),
        name="ffn",
    )(x, mod.table, mod.table, mod.table, mod.norm_g4, mod.norm_g4, wg, wu, wd)
    return (out[0], tuple(out[1:])) if emit else (out[0], None)


def _inproj_stream_kernel(x_ref, sh_ref, sc_ref, gin_ref, w_ref, o_ref, wb_ref, h_ref):
    @pl.when(pl.program_id(0) == 0)
    def _():
        _modulate_rows(x_ref, gin_ref, sc_ref, sh_ref, h_ref)

    wb_ref[...] = w_ref[...].astype(BF16)
    o_ref[...] = _dot(h_ref[...], wb_ref[...])


def _inproj_stream(x, mod, w, li, n_out, cols_per_step):
    m, d = x.shape
    n_cols = w.shape[-1]
    tn = _tile(n_cols, cols_per_step)
    whole = pl.BlockSpec((m, d), lambda j: (0, 0))
    o, wb = pl.pallas_call(
        _inproj_stream_kernel,
        out_shape=[jax.ShapeDtypeStruct((m, n_cols), F32), jax.ShapeDtypeStruct((1, d, n_cols), BF16)],
        grid=(n_cols // tn,),
        in_specs=[whole, mod.whole_spec(0), mod.whole_spec(1), mod.gain_spec(0),
                  pl.BlockSpec((None, d, tn), lambda j: (li, 0, j))],
        out_specs=[pl.BlockSpec((m, tn), lambda j: (0, j)),
                   pl.BlockSpec((None, d, tn), lambda j: (0, 0, j))],
        scratch_shapes=[pltpu.VMEM((m, d), BF16)],
        compiler_params=_params(1),
        name="inproj_stream",
    )(x, mod.table, mod.table, mod.norm_g4, w)
    return jnp.split(o, n_out, axis=1), wb


def _outproj_stream_kernel(y_ref, x_ref, gt_ref, gout_ref, w_ref, o_ref, wb_ref):
    j = pl.program_id(0)

    @pl.when(j == 0)
    def _():
        o_ref[...] = jnp.zeros(o_ref.shape, F32)

    wb_ref[...] = w_ref[...].astype(BF16)
    o_ref[...] += _dot(y_ref[...].astype(BF16), wb_ref[...])

    @pl.when(j == pl.num_programs(0) - 1)
    def _():
        _gated_residual_rows(x_ref, gt_ref, gout_ref, o_ref, 1.0)


def _outproj_stream(y, x, mod, w, li, depth_per_step):
    m, d = x.shape
    k = y.shape[-1]
    tk = _tile(k, depth_per_step)
    whole = pl.BlockSpec((m, d), lambda j: (0, 0))
    return pl.pallas_call(
        _outproj_stream_kernel,
        out_shape=[jax.ShapeDtypeStruct((m, d), F32), jax.ShapeDtypeStruct((1, k, d), BF16)],
        grid=(k // tk,),
        in_specs=[pl.BlockSpec((m, tk), lambda j: (0, j)), whole, mod.whole_spec(2), mod.gain_spec(1),
                  pl.BlockSpec((None, tk, d), lambda j: (li, j, 0))],
        out_specs=[whole, pl.BlockSpec((None, tk, d), lambda j: (0, j, 0))],
        compiler_params=_params(1),
        name="outproj_stream",
    )(y, x, mod.table, mod.norm_g4, w)


def _qkv_kernel(x_ref, sh_ref, sc_ref, gin_ref, w_ref, qb_ref, kb_ref, vb_ref, k_ref, v_ref, h_ref):
    _modulate_rows(x_ref, gin_ref, sc_ref, sh_ref, h_ref)
    d = k_ref.shape[-1]
    qb_ref[...] = (_dot(h_ref[...], w_ref[:, 0:d]) * (HEAD_DIM ** -0.5 * LOG2E)).astype(BF16)
    k_ref[...] = _dot(h_ref[...], w_ref[:, d:2 * d])
    kb_ref[...] = k_ref[...].astype(BF16)
    v_ref[...] = _dot(h_ref[...], w_ref[:, 2 * d:3 * d])
    vb_ref[...] = v_ref[...].astype(BF16)


def _qkv_proj(x, mod, w, li, rows_per_group, tm):
    m, d = x.shape
    out = pl.BlockSpec((tm, d), lambda i: (i, 0))
    return pl.pallas_call(
        _qkv_kernel,
        out_shape=[jax.ShapeDtypeStruct((m, d), BF16)] * 3 + [jax.ShapeDtypeStruct((m, d), F32)] * 2,
        grid=(m // tm,),
        in_specs=[
            pl.BlockSpec((tm, d), lambda i: (i, 0)),
            mod.spec(0, tm, rows_per_group), mod.spec(1, tm, rows_per_group),
            mod.gain_spec(0),
            pl.BlockSpec((None, d, 3 * d), lambda i: (li, 0, 0), pipeline_mode=pl.Buffered(1)),
        ],
        out_specs=[out] * 5,
        scratch_shapes=[pltpu.VMEM((tm, d), BF16)],
        compiler_params=_params(1),
        name="qkv_proj",
    )(x, mod.table, mod.table, mod.norm_g4, w)


def _outproj_kernel(y_ref, x_ref, gt_ref, gout_ref, w_ref, o_ref):
    o_ref[...] = _dot(y_ref[...].astype(BF16), w_ref[...])
    _gated_residual_rows(x_ref, gt_ref, gout_ref, o_ref, 1.0)


def _outproj(y, x, mod, w, li, rows_per_group, tm):
    m, d = x.shape
    k = y.shape[-1]
    return pl.pallas_call(
        _outproj_kernel,
        out_shape=jax.ShapeDtypeStruct((m, d), F32),
        grid=(m // tm,),
        in_specs=[
            pl.BlockSpec((tm, k), lambda i: (i, 0)),
            pl.BlockSpec((tm, d), lambda i: (i, 0)),
            mod.spec(2, tm, rows_per_group),
            mod.gain_spec(1),
            pl.BlockSpec((None, k, d), lambda i: (li, 0, 0), pipeline_mode=pl.Buffered(1)),
        ],
        out_specs=pl.BlockSpec((tm, d), lambda i: (i, 0)),
        compiler_params=_params(1),
        name="outproj",
    )(y, x, mod.table, mod.norm_g4, w)


CONV_ROW_CHUNK = 32
CONV_COL_CHUNK = 512


def _conv_kernel(av_ref, ag_ref, gb_ref, gc_ref, z_ref, ha_ref, hb_ref,
                 dwa_ref, ba_ref, lng_ref, lnb_ref, dwb_ref,
                 y_ref, ca_ref, cb_ref, ubuf, zbuf, cbuf, *, ka, kb):
    s = pl.program_id(1)
    ts, c = av_ref.shape
    pa = ubuf.shape[0] - ts
    pb = zbuf.shape[0] - ts

    @pl.when(s == 0)
    def _():
        ubuf[0:pa - (ka - 1), :] = jnp.zeros((pa - (ka - 1), c), F32)
        ubuf[pa - (ka - 1):pa, :] = ha_ref[...]
        zbuf[0:pb - (kb - 1), :] = jnp.zeros((pb - (kb - 1), c), F32)
        zbuf[pb - (kb - 1):pb, :] = hb_ref[...]

    @pl.when(s > 0)
    def _():
        ubuf[0:pa, :] = ubuf[ts:ts + pa, :]
        zbuf[0:pb, :] = zbuf[ts:ts + pb, :]

    ubuf[pa:, :] = av_ref[...] * jax.nn.sigmoid(ag_ref[...])
    zbuf[pb:, :] = gc_ref[...] * z_ref[...]

    rc = min(CONV_ROW_CHUNK, ts)
    cc = min(CONV_COL_CHUNK, c)
    for r0 in range(0, ts, rc):
        for c0 in range(0, c, cc):
            acc = jnp.zeros((rc, cc), F32)
            for k in range(ka):
                start = r0 + k + pa - (ka - 1)
                acc = acc + dwa_ref[k:k + 1, c0:c0 + cc] * ubuf[start:start + rc, c0:c0 + cc]
            cbuf[r0:r0 + rc, c0:c0 + cc] = acc
    ya = cbuf[...] + ba_ref[...]
    mu = jnp.mean(ya, axis=-1, keepdims=True)
    yc = ya - mu
    ya = yc * lax.rsqrt(jnp.mean(yc * yc, axis=-1, keepdims=True) + EPS) * lng_ref[...] + lnb_ref[...]
    y_ref[:, 0:c] = _silu(ya).astype(y_ref.dtype)

    acc = jnp.zeros((ts, c), F32)
    for k in range(kb):
        start = k + pb - (kb - 1)
        acc = acc + dwb_ref[k:k + 1, :] * zbuf[start:start + ts, :]
    y_ref[:, c:2 * c] = (gb_ref[...] * acc).astype(y_ref.dtype)

    @pl.when(s == pl.num_programs(1) - 1)
    def _():
        ca_ref[...] = ubuf[ts + pa - (ka - 1):ts + pa, :]
        cb_ref[...] = zbuf[ts + pb - (kb - 1):ts + pb, :]


def _conv_mix(proj, hist_a, hist_b, dw_a, b_a, ln_g, ln_b, dw_b, li, n, ts):
    c = proj[0].shape[-1]
    s = proj[0].shape[0] // n
    ka, kb = dw_a.shape[1], dw_b.shape[1]
    pa = -(-(ka - 1) // SUBLANES) * SUBLANES
    pb = -(-(kb - 1) // SUBLANES) * SUBLANES
    tile = pl.BlockSpec((None, ts, c), lambda b, t: (b, t, 0))
    vec = pl.BlockSpec((None, 1, c), lambda b, t: (li, 0, 0))
    in_specs = [tile] * 5
    args = [p.reshape(n, s, c) for p in proj]
    in_specs += [pl.BlockSpec((None, None, ka - 1, c), lambda b, t: (li, b, 0, 0)),
                 pl.BlockSpec((None, None, kb - 1, c), lambda b, t: (li, b, 0, 0))]
    args += [hist_a, hist_b]
    in_specs += [pl.BlockSpec((None, ka, c), lambda b, t: (li, 0, 0)), vec, vec, vec,
                 pl.BlockSpec((None, kb, c), lambda b, t: (li, 0, 0))]
    nl = dw_a.shape[0]
    args += [dw_a, b_a.reshape(nl, 1, c), ln_g.reshape(nl, 1, c), ln_b.reshape(nl, 1, c), dw_b]
    y, ca, cb = pl.pallas_call(
        functools.partial(_conv_kernel, ka=ka, kb=kb),
        out_shape=[jax.ShapeDtypeStruct((n, s, 2 * c), F32),
                   jax.ShapeDtypeStruct((n, ka - 1, c), F32),
                   jax.ShapeDtypeStruct((n, kb - 1, c), F32)],
        grid=(n, s // ts),
        in_specs=in_specs,
        out_specs=[pl.BlockSpec((None, ts, 2 * c), lambda b, t: (b, t, 0)),
                   pl.BlockSpec((None, ka - 1, c), lambda b, t: (b, 0, 0)),
                   pl.BlockSpec((None, kb - 1, c), lambda b, t: (b, 0, 0))],
        scratch_shapes=[pltpu.VMEM((ts + pa, c), F32), pltpu.VMEM((ts + pb, c), F32),
                        pltpu.VMEM((ts, c), F32)],
        compiler_params=_params(2),
        name="conv_mix",
    )(*args)
    return y.reshape(n * s, 2 * c), ca, cb


SUBLANES = 8


def _conv_layer_kernel(x_ref, sh_ref, sc_ref, gt_ref, gin_ref, gout_ref, win_ref,
                       dwa_ref, ba_ref, lng_ref, lnb_ref, dwb_ref, wout_ref,
                       o_ref, ca_ref, cb_ref,
                       h_ref, ubuf, zbuf, gb_ref, xs_ref, cbuf, y_ref, *, ka, kb):
    s = pl.program_id(1)
    ts = x_ref.shape[0]
    c = ubuf.shape[1]
    pa = ubuf.shape[0] - ts
    pb = zbuf.shape[0] - ts
    cc = xs_ref.shape[2]
    n_shift_rows = xs_ref.shape[1]

    @pl.when(s == 0)
    def _():
        ubuf[0:pa, :] = jnp.zeros((pa, c), F32)
        zbuf[0:pb, :] = jnp.zeros((pb, c), F32)

    @pl.when(s > 0)
    def _():
        ubuf[0:pa, :] = ubuf[ts:ts + pa, :]
        zbuf[0:pb, :] = zbuf[ts:ts + pb, :]

    _modulate_rows(x_ref, gin_ref, sc_ref, sh_ref, h_ref)

    def proj(part, c0):
        return _dot(h_ref[...], win_ref[:, part * c + c0:part * c + c0 + cc])

    for c0 in range(0, c, cc):
        ubuf[pa:, c0:c0 + cc] = proj(0, c0) * jax.nn.sigmoid(proj(1, c0))

    for c0 in range(0, c, cc):
        cols = slice(c0, c0 + cc)
        for b in range(1, SUBLANES):
            xs_ref[b - 1] = ubuf[b:b + n_shift_rows, cols]
        for r0 in range(0, ts, CONV_ROW_CHUNK):
            acc = jnp.zeros((CONV_ROW_CHUNK, cc), F32)
            for k in range(ka):
                a, b = divmod(k + pa - (ka - 1), SUBLANES)
                start = a * SUBLANES + r0
                if b == 0:
                    src = ubuf[start:start + CONV_ROW_CHUNK, cols]
                else:
                    src = xs_ref[b - 1, start:start + CONV_ROW_CHUNK, :]
                acc = acc + dwa_ref[k:k + 1, cols] * src
            cbuf[r0:r0 + CONV_ROW_CHUNK, cols] = acc + ba_ref[:, cols]
        gb_ref[:, cols] = proj(2, c0)
        zbuf[pb:, cols] = proj(3, c0) * proj(4, c0)

    for r0 in range(0, ts, ROW_CHUNK):
        rows = slice(r0, r0 + ROW_CHUNK)
        ya = cbuf[rows, :]
        yc = ya - jnp.mean(ya, axis=-1, keepdims=True)
        yn = yc * _inv_rms(yc) * lng_ref[...] + lnb_ref[...]
        y_ref[rows, 0:c] = _silu(yn).astype(BF16)
        acc = jnp.zeros((ROW_CHUNK, c), F32)
        for k in range(kb):
            start = r0 + k + pb - (kb - 1)
            acc = acc + dwb_ref[k:k + 1, :] * zbuf[start:start + ROW_CHUNK, :]
        y_ref[rows, c:2 * c] = (gb_ref[rows, :] * acc).astype(BF16)

    o_ref[...] = _dot(y_ref[...], wout_ref[...])
    _gated_residual_rows(x_ref, gt_ref, gout_ref, o_ref, 1.0)

    @pl.when(s == pl.num_programs(1) - 1)
    def _():
        ca_ref[...] = ubuf[ts + pa - (ka - 1):ts + pa, :]
        cb_ref[...] = zbuf[ts + pb - (kb - 1):ts + pb, :]


def _conv_layer(x, mod, w_in, w_out, dw_a, b_a, ln_g, ln_b, dw_b, li, n, ts):
    m, d = x.shape
    s = m // n
    c = dw_a.shape[-1]
    ka, kb = dw_a.shape[1], dw_b.shape[1]
    pa = -(-(ka - 1) // SUBLANES) * SUBLANES
    pb = -(-(kb - 1) // SUBLANES) * SUBLANES
    cc = min(CONV_COL_CHUNK, c)
    nl = dw_a.shape[0]
    tile = pl.BlockSpec((None, ts, d), lambda b, t: (b, t, 0))
    vec = pl.BlockSpec((None, 1, c), lambda b, t: (li, 0, 0))
    resident = functools.partial(pl.BlockSpec, pipeline_mode=pl.Buffered(1))
    o, ca, cb = pl.pallas_call(
        functools.partial(_conv_layer_kernel, ka=ka, kb=kb),
        out_shape=[jax.ShapeDtypeStruct((n, s, d), F32),
                   jax.ShapeDtypeStruct((n, ka - 1, c), F32),
                   jax.ShapeDtypeStruct((n, kb - 1, c), F32)],
        grid=(n, s // ts),
        in_specs=[tile, mod.group_spec(0), mod.group_spec(1), mod.group_spec(2),
                  mod.gain_spec(0), mod.gain_spec(1),
                  resident((None, d, w_in.shape[-1]), lambda b, t: (0, 0, 0)),
                  pl.BlockSpec((None, ka, c), lambda b, t: (li, 0, 0)), vec, vec, vec,
                  pl.BlockSpec((None, kb, c), lambda b, t: (li, 0, 0)),
                  resident((None, 2 * c, d), lambda b, t: (0, 0, 0))],
        out_specs=[tile,
                   pl.BlockSpec((None, ka - 1, c), lambda b, t: (b, 0, 0)),
                   pl.BlockSpec((None, kb - 1, c), lambda b, t: (b, 0, 0))],
        scratch_shapes=[pltpu.VMEM((ts, d), BF16),
                        pltpu.VMEM((ts + pa, c), F32), pltpu.VMEM((ts + pb, c), F32),
                        pltpu.VMEM((ts, c), F32),
                        pltpu.VMEM((SUBLANES - 1, ts + pa - SUBLANES, cc), F32),
                        pltpu.VMEM((ts, c), F32), pltpu.VMEM((ts, 2 * c), BF16)],
        compiler_params=_params(2),
        name="conv_layer",
    )(x.reshape(n, s, d), mod.table, mod.table, mod.table, mod.norm_g4, mod.norm_g4, w_in,
      dw_a, b_a.reshape(nl, 1, c), ln_g.reshape(nl, 1, c), ln_b.reshape(nl, 1, c), dw_b, w_out)
    return o.reshape(m, d), ca, cb


def _multiplicity(dist, xp):
    cnt = 0
    for window, dil in DIL_PATTERNS:
        assert dil & (dil - 1) == 0
        hit = (dist >= 0) & (dist <= window) & ((dist & (dil - 1)) == 0)
        cnt = cnt + hit.astype(xp.int32)
    return cnt


def _log_multiplicity(dist):
    cnt = _multiplicity(dist, jnp)
    return jnp.where(cnt > 0, jnp.log(jnp.maximum(cnt, 1).astype(F32)), NEG)


def _attn_prompt_kernel(q_ref, k_ref, v_ref, slope_ref, o_ref,
                        vb_ref, s_ref, p_ref, bias_ref, logc_ref, dist_ref, *, tq, seqs):
    dh = q_ref.shape[1]
    s_len = q_ref.shape[0] // seqs
    nq = s_len // tq

    @pl.when((pl.program_id(0) == 0) & (pl.program_id(1) == 0))
    def _():
        dist = (lax.broadcasted_iota(jnp.int32, (tq, s_len), 0) + (s_len - tq)
                - lax.broadcasted_iota(jnp.int32, (tq, s_len), 1))
        logc_ref[...] = _log_multiplicity(dist) * LOG2E
        dist_ref[...] = dist.astype(F32) * LOG2E
        vb_ref[:, dh:] = jnp.ones((seqs * s_len, dh), BF16)

    @pl.when(pl.program_id(1) == 0)
    def _():
        bias_ref[...] = logc_ref[...] - slope_ref[...] * dist_ref[...]

    vb_ref[:, :dh] = v_ref[...]

    units = [(b, i) for b in range(seqs) for i in range(nq)]

    def scores(u):
        b, i = units[u]
        r0 = b * s_len
        c0 = (nq - 1 - i) * tq
        q = q_ref[r0 + i * tq:r0 + (i + 1) * tq, :]
        m = jnp.full((tq, 1), NEG, F32)
        for j in range(i + 1):
            keys = slice(j * tq, (j + 1) * tq)
            s = (_dot_nt(q, k_ref[r0 + j * tq:r0 + (j + 1) * tq, :])
                 + bias_ref[:, c0 + j * tq:c0 + (j + 1) * tq])
            s_ref[u % 2, :, keys] = s
            m = jnp.maximum(m, jnp.max(s, axis=-1, keepdims=True))
        return m

    m_next = scores(0)
    for u, (b, i) in enumerate(units):
        slot, m = u % 2, m_next
        if u + 1 < len(units):
            m_next = scores(u + 1)
        for j in range(i + 1):
            keys = slice(j * tq, (j + 1) * tq)
            p_ref[slot, :, keys] = jnp.exp2(s_ref[slot, :, keys] - m).astype(BF16)
        n_keys = (i + 1) * tq
        r0 = b * s_len
        ol = _dot(p_ref[slot, :, :n_keys], vb_ref[r0:r0 + n_keys, :])
        o_ref[r0 + i * tq:r0 + (i + 1) * tq, :] = (ol[:, :dh] / ol[:, dh:]).astype(o_ref.dtype)


def _alibi_slopes(n_heads):
    return jnp.exp2(-8.0 * jnp.arange(1, n_heads + 1, dtype=F32) / n_heads)


def _attn_prompt(q, k, v, n, tq, seqs):
    m, d = q.shape
    s = m // n
    n_heads = d // HEAD_DIM
    assert n % seqs == 0
    slopes = jnp.broadcast_to(_alibi_slopes(n_heads)[:, None, None], (n_heads, 1, s))
    head = pl.BlockSpec((seqs * s, HEAD_DIM), lambda h, b: (b, h))
    return pl.pallas_call(
        functools.partial(_attn_prompt_kernel, tq=tq, seqs=seqs),
        out_shape=jax.ShapeDtypeStruct((m, d), BF16),
        grid=(n_heads, n // seqs),
        in_specs=[head, head, head, pl.BlockSpec((None, 1, s), lambda h, b: (h, 0, 0))],
        out_specs=head,
        scratch_shapes=[pltpu.VMEM((seqs * s, 2 * HEAD_DIM), BF16),
                        pltpu.VMEM((2, tq, s), F32), pltpu.VMEM((2, tq, s), BF16),
                        pltpu.VMEM((tq, s), F32), pltpu.VMEM((tq, s), F32), pltpu.VMEM((tq, s), F32)],
        compiler_params=_params(2),
        name="attn_prompt",
    )(q, k, v, slopes)


def _sample_regions(wb, t_new):
    dil_max = max(dil for _, dil in DIL_PATTERNS)
    reach = max(window for window, dil in DIL_PATTERNS if dil < dil_max)
    near = -(-reach // dil_max) * dil_max
    assert wb % dil_max == 0 and wb > near and t_new <= dil_max
    r = np.arange(wb - near)
    skipped = r[(r % dil_max) >= t_new]
    for t in range(t_new):
        assert not _multiplicity(wb + t - skipped, np).any()
    return dil_max, near


def _shift(x, n):
    assert n & (n - 1) == 0
    return x >> (n.bit_length() - 1), x & (n - 1)


def _attn_sample_kernel(q_ref, kfar_ref, vfar_ref, knear_ref, vnear_ref, knew_ref, vnew_ref,
                        o_ref, bfar_ref, bnear_ref, bnew_ref, *, t_new, n_heads, wb, dil_max):
    n_far = kfar_ref.shape[0] * kfar_ref.shape[1] * kfar_ref.shape[2]
    n_near = knear_ref.shape[0]
    dh = q_ref.shape[-1]

    @pl.when(pl.program_id(0) == 0)
    def _():
        def table(ref, row_of_slot):
            i = lax.broadcasted_iota(jnp.int32, ref.shape, 0)
            c = lax.broadcasted_iota(jnp.int32, ref.shape, 1)
            h, t = _shift(i, t_new)
            slot, hk = _shift(c, n_heads)
            dist = wb + t - row_of_slot(slot)
            slope = jnp.exp2(-8.0 * (h + 1).astype(F32) / n_heads)
            ref[...] = jnp.where(h == hk, _log_multiplicity(dist) - slope * dist.astype(F32), NEG)

        def far_row(slot):
            g, p = _shift(slot, t_new)
            return g * dil_max + p

        table(bfar_ref, far_row)
        table(bnear_ref, lambda slot: slot + (wb - n_near // n_heads))
        table(bnew_ref, lambda slot: slot + wb)

    q = (q_ref[...] * (dh ** -0.5)).astype(BF16)
    kfar = kfar_ref[...].reshape(n_far, dh).astype(BF16)
    s_far = _dot_nt(q, kfar) + bfar_ref[...]
    s_near = _dot_nt(q, knear_ref[...].astype(BF16)) + bnear_ref[...]
    s_new = _dot_nt(q, knew_ref[...].astype(BF16)) + bnew_ref[...]
    m = jnp.maximum(jnp.maximum(jnp.max(s_far, axis=-1, keepdims=True),
                                jnp.max(s_near, axis=-1, keepdims=True)),
                    jnp.max(s_new, axis=-1, keepdims=True))
    p_far = jnp.exp(s_far - m)
    p_near = jnp.exp(s_near - m)
    p_new = jnp.exp(s_new - m)
    l = (jnp.sum(p_far, axis=-1, keepdims=True) + jnp.sum(p_near, axis=-1, keepdims=True)
         + jnp.sum(p_new, axis=-1, keepdims=True))
    vfar = vfar_ref[...].reshape(n_far, dh).astype(BF16)
    o = (_dot(p_far.astype(BF16), vfar) + _dot(p_near.astype(BF16), vnear_ref[...].astype(BF16))
         + _dot(p_new.astype(BF16), vnew_ref[...].astype(BF16)))
    o_ref[...] = o / l


def _attn_sample(q, k_new, v_new, cache_k, cache_v, li, nb):
    m, d = q.shape
    t = m // nb
    n_layers, _, wb, n_heads, dh = cache_k.shape
    dil_max, near = _sample_regions(wb, t)
    n_groups = (wb - near) // dil_max
    by_head = lambda a: a.reshape(nb, t, n_heads, dh).transpose(0, 2, 1, 3).reshape(nb, n_heads * t, dh)
    rows = lambda a: a.reshape(nb, t * n_heads, dh)
    far = lambda c: c.reshape(n_layers, nb, wb // dil_max, dil_max, n_heads, dh)
    flat = lambda c: c.reshape(n_layers, nb, wb * n_heads, dh)
    qspec = pl.BlockSpec((None, n_heads * t, dh), lambda b: (b, 0, 0))
    far_spec = pl.BlockSpec((None, None, n_groups, t, n_heads, dh), lambda b: (li, b, 0, 0, 0, 0))
    near_spec = pl.BlockSpec((None, None, near * n_heads, dh), lambda b: (li, b, wb // near - 1, 0))
    assert wb % near == 0
    o = pl.pallas_call(
        functools.partial(_attn_sample_kernel, t_new=t, n_heads=n_heads, wb=wb, dil_max=dil_max),
        out_shape=jax.ShapeDtypeStruct((nb, n_heads * t, dh), F32),
        grid=(nb,),
        in_specs=[qspec, far_spec, far_spec, near_spec, near_spec, qspec, qspec],
        out_specs=qspec,
        scratch_shapes=[pltpu.VMEM((n_heads * t, n_groups * t * n_heads), F32),
                        pltpu.VMEM((n_heads * t, near * n_heads), F32),
                        pltpu.VMEM((n_heads * t, t * n_heads), F32)],
        compiler_params=_params(1),
        name="attn_sample",
    )(by_head(q), far(cache_k), far(cache_v), flat(cache_k), flat(cache_v), rows(k_new), rows(v_new))
    return o.reshape(nb, n_heads, t, dh).transpose(0, 2, 1, 3).reshape(m, d)


def _trunk_sample(x3, mod_table, p, caches, tiles):
    tm, tf = tiles["ffn_rows"], tiles["ffn_hidden"]
    n, s, d = x3.shape
    x = x3.reshape(n * s, d)
    depth = p["norm_g4"].shape[0]
    n_heads = d // HEAD_DIM
    ffn_w = (p["ffn_w_gate"], p["ffn_w_up"], p["ffn_w_down"])
    new_a, new_b, new_k, new_v = [], [], [], []
    rounded = {}
    for layer in range(depth):
        li = layer // 2
        mods = [_Mod(mod_table, True, layer, sub, p["norm_g4"]) for sub in range(3)]
        x, rounded["ffn", layer, 0] = _ffn(x, mods[0], ffn_w, layer, 0, s, tm, tf)
        if layer % 2 == 0:
            proj, rounded["conv_w_in", li] = _inproj_stream(x, mods[1], p["conv_w_in"], li, 5,
                                                            tiles["proj_cols"])
            y, ca, cb = _conv_mix(proj, caches[0], caches[1], p["conv_dw_a"], p["conv_b_a"],
                                  p["conv_ln_g"], p["conv_ln_b"], p["conv_dw_b"], li, n, s)
            new_a.append(ca)
            new_b.append(cb)
            x, rounded["conv_w_out", li] = _outproj_stream(y, x, mods[1], p["conv_w_out"], li,
                                                           tiles["proj_depth"])
        else:
            (q, k, v), rounded["attn_w_qkv", li] = _inproj_stream(x, mods[1], p["attn_w_qkv"], li, 3,
                                                                 tiles["proj_cols"])
            o = _attn_sample(q, k, v, caches[2], caches[3], li, n)
            new_k.append(k.reshape(n, s, n_heads, HEAD_DIM))
            new_v.append(v.reshape(n, s, n_heads, HEAD_DIM))
            x, rounded["attn_w_o", li] = _outproj_stream(o, x, mods[1], p["attn_w_o"], li,
                                                         tiles["proj_depth"])
        x, rounded["ffn", layer, 1] = _ffn(x, mods[2], ffn_w, layer, 1, s, tm, tf)
    outs = (x.reshape(n, s, d), jnp.stack(new_a), jnp.stack(new_b), jnp.stack(new_k), jnp.stack(new_v))
    return outs, rounded


def _trunk_prompt(x3, mod_table, p, rounded, tiles):
    tm, tf, tp = tiles["ffn_rows"], tiles["ffn_hidden"], tiles["proj_rows"]
    n, s, d = x3.shape
    x = x3.reshape(n * s, d)
    depth = p["norm_g4"].shape[0]
    n_heads = d // HEAD_DIM
    keep = min(DIL_PATTERNS[-1][0], s)
    new_a, new_b, new_k, new_v = [], [], [], []
    for layer in range(depth):
        li = layer // 2
        mods = [_Mod(mod_table, False, layer, sub, p["norm_g4"]) for sub in range(3)]
        x, _ = _ffn(x, mods[0], rounded["ffn", layer, 0], 0, 0, s, tm, tf)
        if layer % 2 == 0:
            x, ca, cb = _conv_layer(x, mods[1], rounded["conv_w_in", li], rounded["conv_w_out", li],
                                    p["conv_dw_a"], p["conv_b_a"], p["conv_ln_g"], p["conv_ln_b"],
                                    p["conv_dw_b"], li, n, tiles["conv_rows"])
            new_a.append(ca)
            new_b.append(cb)
        else:
            qb, kb, vb, k, v = _qkv_proj(x, mods[1], rounded["attn_w_qkv", li], 0, s, tp)
            o = _attn_prompt(qb, kb, vb, n, tiles["attn_rows"], tiles["attn_seqs"])
            new_k.append(k.reshape(n, s, n_heads, HEAD_DIM)[:, s - keep:])
            new_v.append(v.reshape(n, s, n_heads, HEAD_DIM)[:, s - keep:])
            x = _outproj(o, x, mods[1], rounded["attn_w_o", li], 0, s, tp)
        x, _ = _ffn(x, mods[2], rounded["ffn", layer, 1], 0, 0, s, tm, tf)
    return x.reshape(n, s, d), jnp.stack(new_a), jnp.stack(new_b), jnp.stack(new_k), jnp.stack(new_v)


def _tile(total, want):
    t = min(total, want)
    while total % t:
        t //= 2
    return t


def kernel(x_prompt, x_sample, cache_conv_a, cache_conv_b, cache_k, cache_v, c_prompt, c_sample,
           w_mod, b_mod, norm_g, ffn_w_gate, ffn_w_up, ffn_w_down, conv_w_in, conv_dw_a, conv_b_a,
           conv_ln_g, conv_ln_b, conv_dw_b, conv_w_out, attn_w_qkv, attn_w_o):
    n, s, d = x_prompt.shape
    nb, t, _ = x_sample.shape
    depth = w_mod.shape[0]

    mod = _mod_table(jnp.concatenate([c_prompt, c_sample], axis=0), w_mod, b_mod,
                     tn=_tile(w_mod.shape[-1], 1024))
    mod_p = mod[:, :n].reshape(depth, n, N_MOD, 1, d)
    mod_s = jnp.repeat(mod[:, n:].reshape(depth, nb, N_MOD, d), t, axis=1).transpose(0, 2, 1, 3)

    p = dict(
        norm_g4=norm_g.reshape(depth, 6, 1, d),
        ffn_w_gate=ffn_w_gate, ffn_w_up=ffn_w_up, ffn_w_down=ffn_w_down,
        conv_w_in=conv_w_in, conv_w_out=conv_w_out, attn_w_qkv=attn_w_qkv, attn_w_o=attn_w_o,
        conv_dw_a=conv_dw_a, conv_b_a=conv_b_a, conv_ln_g=conv_ln_g, conv_ln_b=conv_ln_b,
        conv_dw_b=conv_dw_b,
    )
    tf = _tile(ffn_w_gate.shape[-1], 512)
    tiles_prompt = dict(ffn_rows=_tile(s, 1024), ffn_hidden=tf, proj_rows=_tile(s, 256),
                        conv_rows=_tile(s, 256), attn_rows=_tile(s, 256), attn_seqs=_tile(n, 2))
    tiles_sample = dict(ffn_rows=nb * t, ffn_hidden=tf, proj_cols=1024, proj_depth=512)
    sample_out, rounded = _trunk_sample(x_sample, mod_s, p,
                                        (cache_conv_a, cache_conv_b, cache_k, cache_v), tiles_sample)
    sample_out, rounded, x_prompt, mod_p = lax.optimization_barrier((sample_out, rounded, x_prompt, mod_p))
    ys, as_, bs, ks, vs = sample_out
    yp, ap, bp, kp, vp = _trunk_prompt(x_prompt, mod_p, p, rounded, tiles_prompt)
    return (yp, ys, ap, as_, bp, bs, kp, vp, ks, vs)
```
